```python
import jax, jax.numpy as jnp
from jax import lax
import numpy as np

D_MODEL = 1024
BATCH = 32
SEQ = 2048
DEPTH = 4

CHUNK = 64
SGU_WINDOW = 128
SGU_WIDTH = D_MODEL
SGU_GROUPS = 8
SGU_GROUP_DIM = SGU_WIDTH // SGU_GROUPS
GLA_HEADS = 4
GLA_DK = (D_MODEL // 2) // GLA_HEADS
GLA_DV = D_MODEL // GLA_HEADS
GLA_GATE_RANK = 16
GLA_GATE_NORM = 16.0
N_EXPERTS = 32
TOP_K = 4
D_FF_EXPERT = D_MODEL
SWIGLU_LIMIT = 7.0
SWIGLU_ALPHA = 1.702
PLE_DIM = 256
LN_EPS = 1e-5
DEEPNORM_ALPHA = (2 * DEPTH) ** 0.25
DEEPNORM_BETA = (8 * DEPTH) ** -0.25

IN_SIZES = (SGU_WIDTH, SGU_WIDTH,
            GLA_HEADS * GLA_DK, GLA_HEADS * GLA_DK,
            GLA_HEADS * GLA_DV, GLA_HEADS * GLA_DV,
            GLA_GATE_RANK, D_MODEL, D_MODEL)
D_IN = sum(IN_SIZES)

kernel_name = "hybrid_sgu_gla_moe_deepnorm"


def layer_norm(x, g, b):
    xf = x.astype(jnp.float32)
    mu = jnp.mean(xf, axis=-1, keepdims=True)
    var = jnp.mean(jnp.square(xf - mu), axis=-1, keepdims=True)
    y = (xf - mu) * lax.rsqrt(var + LN_EPS)
    return (y * g.astype(jnp.float32) + b.astype(jnp.float32)).astype(x.dtype)


def sgu_mixer(u, v, ln_g, ln_b, w_s, b_s):
    B, S, W = v.shape
    v = layer_norm(v, ln_g, ln_b)
    n_win = S // SGU_WINDOW
    vg = v.reshape(B, n_win, SGU_WINDOW, SGU_GROUPS, SGU_GROUP_DIM)
    blk = jnp.arange(SGU_WINDOW) // CHUNK
    mask = blk[:, None] >= blk[None, :]
    w = jnp.where(mask[None], w_s, jnp.zeros_like(w_s))
    s = jnp.einsum('gij,bwjgc->bwigc', w, vg) + b_s.T[None, None, :, :, None]
    return u * s.reshape(B, S, W)


def gla_mixer(q, k, v, log_a, r, norm_g):
    dtype = q.dtype
    B, S, _ = q.shape
    nc = S // CHUNK

    def heads(t, d):
        return t.astype(jnp.float32).reshape(B, nc, CHUNK, GLA_HEADS, d).transpose(0, 3, 1, 2, 4)

    qh = heads(q, GLA_DK) * (GLA_DK ** -0.5)
    kh = heads(k, GLA_DK)
    vh = heads(v, GLA_DV)
    b = jnp.cumsum(heads(log_a, GLA_DK), axis=3)
    q_t = qh * jnp.exp(b)
    k_t = kh * jnp.exp(-b)
    causal = jnp.tril(jnp.ones((CHUNK, CHUNK), dtype=bool))
    att = jnp.einsum('bhnid,bhnjd->bhnij', q_t, k_t)
    att = jnp.where(causal, att, 0.0)
    o_intra = jnp.einsum('bhnij,bhnje->bhnie', att, vh)

    b_end = b[:, :, :, -1:, :]
    k_end = kh * jnp.exp(b_end - b)
    dec = jnp.exp(b_end[:, :, :, 0, :])

    def step(state, xs):
        qc, kc, vc, dc = xs
        o = jnp.einsum('bhid,bhde->bhie', qc, state)
        state = dc[..., None] * state + jnp.einsum('bhjd,bhje->bhde', kc, vc)
        return state, o

    to_time = lambda t: jnp.moveaxis(t, 2, 0)
    s0 = jnp.zeros((B, GLA_HEADS, GLA_DK, GLA_DV), jnp.float32)
    _, o_inter = lax.scan(step, s0, (to_time(q_t), to_time(k_end), to_time(vh), to_time(dec)))
    o = o_intra + jnp.moveaxis(o_inter, 0, 2)
    o = o * lax.rsqrt(jnp.mean(jnp.square(o), axis=-1, keepdims=True) + LN_EPS)
    o = o.transpose(0, 2, 3, 1, 4).reshape(B, S, GLA_HEADS * GLA_DV) * norm_g.astype(jnp.float32)
    return (o * jax.nn.silu(r.astype(jnp.float32))).astype(dtype)


def hybrid_mixer(x, w_in, w_gate_a, b_gate_a, sgu_ln_g, sgu_ln_b, sgu_w, sgu_b,
                 gla_norm_g, w_branch_a, w_branch_b, w_out):
    z = x @ w_in
    splits = np.cumsum(IN_SIZES)[:-1].tolist()
    u, v, q, k, val, r, glr, ga, gb = jnp.split(z, splits, axis=-1)
    a = sgu_mixer(jax.nn.gelu(u), jax.nn.gelu(v), sgu_ln_g, sgu_ln_b, sgu_w, sgu_b)
    log_a = jax.nn.log_sigmoid((glr @ w_gate_a + b_gate_a).astype(jnp.float32)) / GLA_GATE_NORM
    o = gla_mixer(q, k, val, log_a, r, gla_norm_g)
    h = jax.nn.sigmoid(ga) * (a @ w_branch_a) + jax.nn.sigmoid(gb) * (o @ w_branch_b)
    return h @ w_out


def moe_ffn(x, w_router, b_router, w_gate_up, b_gate_up, w_down, b_down):
    B, S, D = x.shape
    xt = x.reshape(B * S, D)
    logits = (xt @ w_router + b_router).astype(jnp.float32)
    top_v, top_i = lax.top_k(logits, TOP_K)
    top_w = jax.nn.softmax(top_v, axis=-1)
    combine = jnp.sum(jax.nn.one_hot(top_i, N_EXPERTS, dtype=jnp.float32) * top_w[..., None], axis=1)
    out = jnp.zeros((B * S, D), jnp.float32)
    for e in range(N_EXPERTS):
        h = xt @ w_gate_up[e] + b_gate_up[e]
        gate, up = h[:, :D_FF_EXPERT], h[:, D_FF_EXPERT:]
        gate = jnp.minimum(gate, SWIGLU_LIMIT)
        up = jnp.clip(up, -SWIGLU_LIMIT, SWIGLU_LIMIT)
        act = (up + 1.0) * gate * jax.nn.sigmoid(SWIGLU_ALPHA * gate)
        y = act @ w_down[e] + b_down[e]
        out = out + combine[:, e:e + 1] * y.astype(jnp.float32)
    return out.astype(x.dtype).reshape(B, S, D)


def setup_inputs(seed: int = 0) -> dict:
    key = jax.random.key(seed)
    ks = jax.random.split(key, 25)
    f32 = jnp.float32

    def nrm(k, shape, scale):
        return jax.random.normal(k, shape, f32) * scale

    L, D, E, F = DEPTH, D_MODEL, N_EXPERTS, D_FF_EXPERT
    HK, HV = GLA_HEADS * GLA_DK, GLA_HEADS * GLA_DV
    return {
        "x": nrm(ks[0], (BATCH, SEQ, D), 1.0),
        "p": nrm(ks[1], (L, BATCH, SEQ, PLE_DIM), 1.0),
        "w_in": nrm(ks[2], (L, D, D_IN), D ** -0.5),
        "w_gate_a": nrm(ks[3], (L, GLA_GATE_RANK, HK), GLA_GATE_RANK ** -0.5),
        "b_gate_a": nrm(ks[4], (L, HK), 0.5),
        "sgu_ln_g": 1.0 + nrm(ks[5], (L, SGU_WIDTH), 0.02),
        "sgu_ln_b": nrm(ks[6], (L, SGU_WIDTH), 0.02),
        "sgu_w": nrm(ks[7], (L, SGU_GROUPS, SGU_WINDOW, SGU_WINDOW), SGU_WINDOW ** -0.5),
        "sgu_b": 1.0 + nrm(ks[8], (L, SGU_GROUPS, SGU_WINDOW), 0.02),
        "gla_norm_g": 1.0 + nrm(ks[9], (L, HV), 0.02),
        "w_branch_a": nrm(ks[10], (L, SGU_WIDTH, D), SGU_WIDTH ** -0.5 * DEEPNORM_BETA),
        "w_branch_b": nrm(ks[11], (L, HV, D), HV ** -0.5 * DEEPNORM_BETA),
        "w_out": nrm(ks[12], (L, D, D), D ** -0.5 * DEEPNORM_BETA),
        "ln1_g": 1.0 + nrm(ks[13], (L, D), 0.02),
        "ln1_b": nrm(ks[14], (L, D), 0.02),
        "w_router": nrm(ks[15], (L, D, E), D ** -0.5),
        "b_router": nrm(ks[16], (L, E), 0.01),
        "w_gate_up": nrm(ks[17], (L, E, D, 2 * F), D ** -0.5),
        "b_gate_up": nrm(ks[18], (L, E, 2 * F), 0.02),
        "w_down": nrm(ks[19], (L, E, F, D), F ** -0.5 * DEEPNORM_BETA),
        "b_down": nrm(ks[20], (L, E, D), 0.02),
        "w_ple_proj": nrm(ks[21], (L, PLE_DIM, D), PLE_DIM ** -0.5 * DEEPNORM_BETA),
        "w_ple_gate": nrm(ks[22], (L, D, D), D ** -0.5),
        "ln2_g": 1.0 + nrm(ks[23], (L, D), 0.02),
        "ln2_b": nrm(ks[24], (L, D), 0.02),
    }


def reference(x, p, w_in, w_gate_a, b_gate_a, sgu_ln_g, sgu_ln_b, sgu_w, sgu_b,
              gla_norm_g, w_branch_a, w_branch_b, w_out, ln1_g, ln1_b,
              w_router, b_router, w_gate_up, b_gate_up, w_down, b_down,
              w_ple_proj, w_ple_gate, ln2_g, ln2_b):
    for i in range(DEPTH):
        mix = hybrid_mixer(x, w_in[i], w_gate_a[i], b_gate_a[i], sgu_ln_g[i], sgu_ln_b[i],
                           sgu_w[i], sgu_b[i], gla_norm_g[i], w_branch_a[i], w_branch_b[i], w_out[i])
        x = layer_norm(DEEPNORM_ALPHA * x + mix, ln1_g[i], ln1_b[i])
        ffn = moe_ffn(x, w_router[i], b_router[i], w_gate_up[i], b_gate_up[i], w_down[i], b_down[i])
        ple = jax.nn.sigmoid(x @ w_ple_gate[i]) * (p[i] @ w_ple_proj[i])
        x = layer_norm(DEEPNORM_ALPHA * x + ffn + ple, ln2_g[i], ln2_b[i])
    return x
```

```python
import functools

import jax
import jax.numpy as jnp
from jax import lax
from jax.experimental import pallas as pl
from jax.experimental.pallas import tpu as pltpu

D_MODEL = 1024
DEPTH = 4
CHUNK = 64
SGU_WINDOW = 128
SGU_GROUPS = 8
GLA_HEADS = 4
GLA_DK = 128
GLA_DV = 256
GLA_GATE_RANK = 16
GLA_GATE_NORM = 16.0
N_EXPERTS = 32
TOP_K = 4
D_FF = 1024
SWIGLU_LIMIT = 7.0
SWIGLU_ALPHA = 1.702
PLE_DIM = 256
LN_EPS = 1e-5
DEEPNORM_ALPHA = (2 * DEPTH) ** 0.25
HK = GLA_HEADS * GLA_DK
HV = GLA_HEADS * GLA_DV
IN_SIZES = (D_MODEL, D_MODEL, HK, HK, HV, HV, GLA_GATE_RANK, D_MODEL, D_MODEL)

ROW_TILE = 256
GLA_TILE = 512
EXPERT_TILE = 256
VMEM_LIMIT = 56 * 1024 * 1024

F32 = jnp.float32
BF16 = jnp.bfloat16


def _dot(a, b, **kw):
    return jnp.dot(a, b, preferred_element_type=F32, **kw)


def _layer_norm(y, g, b):
    mu = jnp.mean(y, axis=-1, keepdims=True)
    d = y - mu
    var = jnp.mean(d * d, axis=-1, keepdims=True)
    return d * lax.rsqrt(var + LN_EPS) * g + b


def _params(*sem):
    return pltpu.CompilerParams(dimension_semantics=sem, vmem_limit_bytes=VMEM_LIMIT)


def _rows(tm, n):
    return pl.BlockSpec((tm, n), lambda i: (i, 0))


def _full(shape):
    nd = len(shape)
    return pl.BlockSpec(shape, lambda i: (0,) * nd)


def _inproj_kernel(x_ref, wu_ref, wv_ref, wq_ref, wk_ref, wval_ref, wr_ref, wglr_ref, wga_ref, wgb_ref,
                   wgate_ref, bgate_ref, lng_ref, lnb_ref,
                   gu_ref, vn_ref, q_ref, k_ref, val_ref, rs_ref, la_ref, sga_ref, sgb_ref):
    x = x_ref[...]
    gu_ref[...] = jax.nn.gelu(_dot(x, wu_ref[...])).astype(BF16)
    v = jax.nn.gelu(_dot(x, wv_ref[...]))
    vn_ref[...] = _layer_norm(v, lng_ref[...], lnb_ref[...]).astype(BF16)
    q_ref[...] = _dot(x, wq_ref[...]).astype(BF16)
    k_ref[...] = _dot(x, wk_ref[...]).astype(BF16)
    val_ref[...] = _dot(x, wval_ref[...]).astype(BF16)
    r = _dot(x, wr_ref[...])
    rs_ref[...] = (r * jax.nn.sigmoid(r)).astype(BF16)
    glr = _dot(x, wglr_ref[...])
    z = _dot(glr, wgate_ref[...], precision=lax.Precision.HIGHEST) + bgate_ref[...]
    la_ref[...] = (jnp.minimum(z, 0.0) - jnp.log(1.0 + jnp.exp(-jnp.abs(z)))) * (1.0 / GLA_GATE_NORM)
    sga_ref[...] = jax.nn.sigmoid(_dot(x, wga_ref[...])).astype(BF16)
    sgb_ref[...] = jax.nn.sigmoid(_dot(x, wgb_ref[...])).astype(BF16)


def _inproj(xb, w_in, w_gate_a, b_gate_a, ln_g, ln_b):
    T, D = xb.shape
    tm = min(ROW_TILE, T)
    splits = []
    o = 0
    for n in IN_SIZES:
        splits.append(w_in[:, o:o + n].astype(BF16))
        o += n
    wu, wv, wq, wk, wval, wr, wglr, wga, wgb = splits
    weights = [wu, wv, wq, wk, wval, wr, wglr, wga, wgb,
               w_gate_a.astype(F32), b_gate_a.reshape(1, HK).astype(F32),
               ln_g.reshape(1, D).astype(F32), ln_b.reshape(1, D).astype(F32)]
    out_shapes = [
        jax.ShapeDtypeStruct((T, D), BF16),
        jax.ShapeDtypeStruct((T, D), BF16),
        jax.ShapeDtypeStruct((T, HK), BF16),
        jax.ShapeDtypeStruct((T, HK), BF16),
        jax.ShapeDtypeStruct((T, HV), BF16),
        jax.ShapeDtypeStruct((T, HV), BF16),
        jax.ShapeDtypeStruct((T, HK), F32),
        jax.ShapeDtypeStruct((T, D), BF16),
        jax.ShapeDtypeStruct((T, D), BF16),
    ]
    return pl.pallas_call(
        _inproj_kernel,
        grid=(T // tm,),
        in_specs=[_rows(tm, D)] + [_full(w.shape) for w in weights],
        out_specs=[_rows(tm, s.shape[1]) for s in out_shapes],
        out_shape=out_shapes,
        compiler_params=_params("parallel"),
        name="inproj",
    )(xb, *weights)


def _gla_kernel(q_ref, k_ref, v_ref, la_ref, rs_ref, g_ref, o_ref, st_ref, *, n_chunks):
    @pl.when(pl.program_id(1) == 0)
    def _():
        st_ref[...] = jnp.zeros_like(st_ref)

    row = lax.broadcasted_iota(jnp.int32, (CHUNK, CHUNK), 0)
    col = lax.broadcasted_iota(jnp.int32, (CHUNK, CHUNK), 1)
    causal = row >= col
    tri = causal.astype(F32)
    nt = (((1,), (1,)), ((), ()))
    tn = (((0,), (0,)), ((), ()))

    def chunk(c, carry):
        r0 = pl.multiple_of(c * CHUNK, CHUNK)
        rows = pl.ds(r0, CHUNK)
        for h in range(GLA_HEADS):
            kc = slice(h * GLA_DK, (h + 1) * GLA_DK)
            vc = slice(h * GLA_DV, (h + 1) * GLA_DV)
            la = la_ref[rows, kc]
            b = _dot(tri, la, precision=lax.Precision.HIGHEST)
            b_end = b[CHUNK - 1:CHUNK, :]
            qh = q_ref[rows, kc].astype(F32) * (GLA_DK ** -0.5)
            kh = k_ref[rows, kc].astype(F32)
            qt = (qh * jnp.exp(b)).astype(BF16)
            kt = (kh * jnp.exp(-b)).astype(BF16)
            k_end = (kh * jnp.exp(b_end - b)).astype(BF16)
            vh = v_ref[rows, vc]
            att = lax.dot_general(qt, kt, nt, preferred_element_type=F32)
            att = jnp.where(causal, att, 0.0).astype(BF16)
            st = st_ref[h]
            o = _dot(att, vh) + lax.dot_general(qt, st.astype(BF16), nt, preferred_element_type=F32)
            st_ref[h] = jnp.exp(b_end) * st + lax.dot_general(vh, k_end, tn, preferred_element_type=F32)
            o = o * lax.rsqrt(jnp.mean(o * o, axis=-1, keepdims=True) + LN_EPS)
            o = o * g_ref[:, vc] * rs_ref[rows, vc].astype(F32)
            o_ref[rows, vc] = o.astype(BF16)
        return carry

    lax.fori_loop(0, n_chunks, chunk, 0)


def _gla(q, k, v, la, rs, norm_g, batch, seq):
    T = q.shape[0]
    ts = min(GLA_TILE, seq)
    ns = seq // ts

    def rows(n):
        return pl.BlockSpec((ts, n), lambda b, s: (b * ns + s, 0))

    return pl.pallas_call(
        functools.partial(_gla_kernel, n_chunks=ts // CHUNK),
        grid=(batch, ns),
        in_specs=[rows(HK), rows(HK), rows(HV), rows(HK), rows(HV),
                  pl.BlockSpec((1, HV), lambda b, s: (0, 0))],
        out_specs=rows(HV),
        out_shape=jax.ShapeDtypeStruct((T, HV), BF16),
        scratch_shapes=[pltpu.VMEM((GLA_HEADS, GLA_DV, GLA_DK), F32)],
        compiler_params=_params("parallel", "arbitrary"),
        name="gla",
    )(q, k, v, la, rs, norm_g.reshape(1, HV).astype(F32))


def _merge_kernel(x_ref, gu_ref, vn_ref, o_ref, sga_ref, sgb_ref, ws_ref, bs_ref, wba_ref, wbb_ref, wout_ref,
                  g_ref, b_ref, x1_ref, x1b_ref, a_ref, *, n_win):
    row = lax.broadcasted_iota(jnp.int32, (SGU_WINDOW, SGU_WINDOW), 0) // CHUNK
    col = lax.broadcasted_iota(jnp.int32, (SGU_WINDOW, SGU_WINDOW), 1) // CHUNK
    keep = row >= col
    gd = D_MODEL // SGU_GROUPS
    for g in range(SGU_GROUPS):
        wm = jnp.where(keep, ws_ref[g], 0.0).astype(BF16)
        bias = bs_ref[:, g:g + 1]
        cs = slice(g * gd, (g + 1) * gd)
        for w in range(n_win):
            rs = slice(w * SGU_WINDOW, (w + 1) * SGU_WINDOW)
            s = _dot(wm, vn_ref[rs, cs]) + bias
            a_ref[rs, cs] = (gu_ref[rs, cs].astype(F32) * s).astype(BF16)
    h = (sga_ref[...].astype(F32) * _dot(a_ref[...], wba_ref[...])
         + sgb_ref[...].astype(F32) * _dot(o_ref[...], wbb_ref[...]))
    mix = _dot(h.astype(BF16), wout_ref[...])
    x1 = _layer_norm(DEEPNORM_ALPHA * x_ref[...] + mix, g_ref[...], b_ref[...])
    x1_ref[...] = x1
    x1b_ref[...] = x1.astype(BF16)


def _merge(x, gu, vn, o, sga, sgb, sgu_w, sgu_b, w_ba, w_bb, w_out, ln_g, ln_b):
    T, D = x.shape
    tm = min(ROW_TILE, T)
    weights = [sgu_w.astype(F32), sgu_b.T.astype(F32), w_ba.astype(BF16), w_bb.astype(BF16), w_out.astype(BF16),
               ln_g.reshape(1, D).astype(F32), ln_b.reshape(1, D).astype(F32)]
    return pl.pallas_call(
        functools.partial(_merge_kernel, n_win=tm // SGU_WINDOW),
        grid=(T // tm,),
        in_specs=[_rows(tm, D)] * 6 + [_full(w.shape) for w in weights],
        out_specs=[_rows(tm, D), _rows(tm, D)],
        out_shape=[jax.ShapeDtypeStruct((T, D), F32), jax.ShapeDtypeStruct((T, D), BF16)],
        scratch_shapes=[pltpu.VMEM((tm, D), BF16)],
        compiler_params=_params("parallel"),
        name="merge",
    )(x, gu, vn, o, sga, sgb, *weights)


def _router_kernel(x_ref, wr_ref, br_ref, idx_ref, rank_ref, wt_ref, cnt_ref, carry_ref, *, tm):
    @pl.when(pl.program_id(0) == 0)
    def _():
        carry_ref[...] = jnp.zeros_like(carry_ref)

    logits = _dot(x_ref[...], wr_ref[...], precision=lax.Precision.HIGHEST) + br_ref[...]
    lane = lax.broadcasted_iota(jnp.int32, (tm, N_EXPERTS), 1)
    work = logits
    hots, vals = [], []
    for _ in range(TOP_K):
        m = jnp.max(work, axis=-1, keepdims=True)
        first = jnp.min(jnp.where(work == m, lane, N_EXPERTS), axis=-1, keepdims=True)
        hot = lane == first
        hots.append(hot)
        vals.append(m)
        work = jnp.where(hot, -jnp.inf, work)
    exps = [jnp.exp(v - vals[0]) for v in vals]
    denom = exps[0] + exps[1] + exps[2] + exps[3]

    mask = (hots[0] | hots[1] | hots[2] | hots[3]).astype(F32)
    r = lax.broadcasted_iota(jnp.int32, (tm, tm), 0)
    c = lax.broadcasted_iota(jnp.int32, (tm, tm), 1)
    below = (r > c).astype(BF16)
    rank_all = _dot(below, mask.astype(BF16)) + carry_ref[...]
    carry_ref[...] += jnp.sum(mask, axis=0, keepdims=True)
    cnt_ref[...] = carry_ref[...]

    for j in range(TOP_K):
        idx_ref[:, j:j + 1] = jnp.sum(jnp.where(hots[j], lane, 0), axis=-1, keepdims=True)
        rank_ref[:, j:j + 1] = jnp.sum(jnp.where(hots[j], rank_all, 0.0), axis=-1, keepdims=True).astype(jnp.int32)
        wt_ref[:, j:j + 1] = exps[j] / denom


def _router(x1, w_router, b_router):
    T, D = x1.shape
    tm = min(ROW_TILE, T)
    return pl.pallas_call(
        functools.partial(_router_kernel, tm=tm),
        grid=(T // tm,),
        in_specs=[_rows(tm, D), _full((D, N_EXPERTS)), _full((1, N_EXPERTS))],
        out_specs=[_rows(tm, TOP_K), _rows(tm, TOP_K), _rows(tm, TOP_K), _full((1, N_EXPERTS))],
        out_shape=[jax.ShapeDtypeStruct((T, TOP_K), jnp.int32), jax.ShapeDtypeStruct((T, TOP_K), jnp.int32),
                   jax.ShapeDtypeStruct((T, TOP_K), F32), jax.ShapeDtypeStruct((1, N_EXPERTS), F32)],
        scratch_shapes=[pltpu.VMEM((1, N_EXPERTS), F32)],
        compiler_params=_params("arbitrary"),
        name="router",
    )(x1, w_router.astype(F32), b_router.reshape(1, N_EXPERTS).astype(F32))


def _slot_kernel(off_ref, idx_ref, rank_ref, slot_ref):
    idx = idx_ref[...]
    acc = rank_ref[...]
    for e in range(N_EXPERTS):
        acc = acc + jnp.where(idx == e, off_ref[e], 0)
    slot_ref[...] = acc


def _slots(offsets, idx, rank):
    T = idx.shape[0]
    tm = min(ROW_TILE, T)
    spec = pl.BlockSpec((tm, TOP_K), lambda i, off: (i, 0))
    return pl.pallas_call(
        _slot_kernel,
        grid_spec=pltpu.PrefetchScalarGridSpec(
            num_scalar_prefetch=1, grid=(T // tm,), in_specs=[spec, spec], out_specs=spec),
        out_shape=jax.ShapeDtypeStruct((T, TOP_K), jnp.int32),
        compiler_params=_params("parallel"),
        name="slots",
    )(offsets, idx, rank)


def _dispatch_kernel(slot_ref, x_ref, xs_ref, sem, *, tm):
    def row_copy(t, j):
        return pltpu.make_async_copy(
            x_ref.at[pl.ds(t, 1), :], xs_ref.at[pl.ds(slot_ref[0, 0, t * TOP_K + j], 1), :], sem)

    def issue(t, carry):
        for j in range(TOP_K):
            row_copy(t, j).start()
        return carry

    lax.fori_loop(0, tm, issue, 0)

    def drain(t, carry):
        for j in range(TOP_K):
            row_copy(t, j).wait()
        return carry

    lax.fori_loop(0, tm, drain, 0)


def _dispatch(slot3, x1, n_slots):
    T, D = x1.shape
    tm = slot3.shape[2] // TOP_K
    return pl.pallas_call(
        functools.partial(_dispatch_kernel, tm=tm),
        grid=(T // tm,),
        in_specs=[pl.BlockSpec((1, 1, tm * TOP_K), lambda i: (i, 0, 0), memory_space=pltpu.SMEM),
                  _rows(tm, D)],
        out_specs=pl.BlockSpec(memory_space=pl.ANY),
        out_shape=jax.ShapeDtypeStruct((n_slots, D), F32),
        scratch_shapes=[pltpu.SemaphoreType.DMA(())],
        compiler_params=_params("arbitrary"),
        name="dispatch",
    )(slot3, x1)


def _expert_kernel(te_ref, tv_ref, tb_ref, xs_ref, wgu_ref, bgu_ref, wd_ref, bd_ref, y_ref, *, te):
    j = pl.program_id(0)
    valid = tv_ref[j]

    @pl.when(valid > 0)
    def _():
        row = lax.broadcasted_iota(jnp.int32, (te, 1), 0)
        x = jnp.where(row < valid, xs_ref[...], 0.0).astype(BF16)
        h = _dot(x, wgu_ref[...]) + bgu_ref[...]
        gate = jnp.minimum(h[:, :D_FF], SWIGLU_LIMIT)
        up = jnp.clip(h[:, D_FF:], -SWIGLU_LIMIT, SWIGLU_LIMIT)
        act = (up + 1.0) * gate * jax.nn.sigmoid(SWIGLU_ALPHA * gate)
        y_ref[...] = _dot(act.astype(BF16), wd_ref[...]) + bd_ref[...]


def _experts(tile_expert, tile_valid, tile_block, xs, w_gate_up, b_gate_up, w_down, b_down):
    n_slots, D = xs.shape
    te = EXPERT_TILE
    n_tiles = n_slots // te
    return pl.pallas_call(
        functools.partial(_expert_kernel, te=te),
        grid_spec=pltpu.PrefetchScalarGridSpec(
            num_scalar_prefetch=3,
            grid=(n_tiles,),
            in_specs=[
                pl.BlockSpec((te, D), lambda j, e, v, b: (b[j], 0)),
                pl.BlockSpec((None, D, 2 * D_FF), lambda j, e, v, b: (e[j], 0, 0)),
                pl.BlockSpec((None, 1, 2 * D_FF), lambda j, e, v, b: (e[j], 0, 0)),
                pl.BlockSpec((None, D_FF, D), lambda j, e, v, b: (e[j], 0, 0)),
                pl.BlockSpec((None, 1, D), lambda j, e, v, b: (e[j], 0, 0)),
            ],
            out_specs=pl.BlockSpec((te, D), lambda j, e, v, b: (b[j], 0)),
        ),
        out_shape=jax.ShapeDtypeStruct((n_slots, D), F32),
        compiler_params=_params("arbitrary"),
        name="experts",
    )(tile_expert, tile_valid, tile_block, xs,
      w_gate_up.astype(BF16), b_gate_up.reshape(N_EXPERTS, 1, 2 * D_FF).astype(F32),
      w_down.astype(BF16), b_down.reshape(N_EXPERTS, 1, D).astype(F32))


def _combine_kernel(slot_ref, x1_ref, x1b_ref, p_ref, wt_ref, wpg_ref, wpp_ref, g_ref, b_ref, y_ref,
                    x2_ref, x2b_ref, ybuf, sem, *, tm):
    def row_copy(t, j):
        return pltpu.make_async_copy(
            y_ref.at[pl.ds(slot_ref[0, 0, t * TOP_K + j], 1), :], ybuf.at[j, pl.ds(t, 1), :], sem)

    def issue(t, carry):
        for j in range(TOP_K):
            row_copy(t, j).start()
        return carry

    lax.fori_loop(0, tm, issue, 0)

    ple = jax.nn.sigmoid(_dot(x1b_ref[...], wpg_ref[...])) * _dot(p_ref[...].astype(BF16), wpp_ref[...])
    acc = DEEPNORM_ALPHA * x1_ref[...] + ple

    def drain(t, carry):
        for j in range(TOP_K):
            row_copy(t, j).wait()
        return carry

    lax.fori_loop(0, tm, drain, 0)

    for j in range(TOP_K):
        acc = acc + wt_ref[:, j:j + 1] * ybuf[j]
    x2 = _layer_norm(acc, g_ref[...], b_ref[...])
    x2_ref[...] = x2
    x2b_ref[...] = x2.astype(BF16)


def _combine(slot3, x1, x1b, p, wt, w_ple_gate, w_ple_proj, ln_g, ln_b, y):
    T, D = x1.shape
    tm = slot3.shape[2] // TOP_K
    weights = [w_ple_gate.astype(BF16), w_ple_proj.astype(BF16),
               ln_g.reshape(1, D).astype(F32), ln_b.reshape(1, D).astype(F32)]
    return pl.pallas_call(
        functools.partial(_combine_kernel, tm=tm),
        grid=(T // tm,),
        in_specs=[pl.BlockSpec((1, 1, tm * TOP_K), lambda i: (i, 0, 0), memory_space=pltpu.SMEM),
                  _rows(tm, D), _rows(tm, D), _rows(tm, PLE_DIM), _rows(tm, TOP_K)]
                 + [_full(w.shape) for w in weights]
                 + [pl.BlockSpec(memory_space=pl.ANY)],
        out_specs=[_rows(tm, D), _rows(tm, D)],
        out_shape=[jax.ShapeDtypeStruct((T, D), F32), jax.ShapeDtypeStruct((T, D), BF16)],
        scratch_shapes=[pltpu.VMEM((TOP_K, tm, D), F32), pltpu.SemaphoreType.DMA(())],
        compiler_params=_params("arbitrary"),
        name="combine",
    )(slot3, x1, x1b, p, wt, *weights, y)


def _tile_plan(counts, n_tiles, te):
    counts = counts.reshape(N_EXPERTS).astype(jnp.int32)
    padded = ((counts + te - 1) // te) * te
    ends = jnp.cumsum(padded)
    offsets = ends - padded
    n_used = ends[-1] // te
    tile = jnp.arange(n_tiles, dtype=jnp.int32)
    block = jnp.minimum(tile, n_used - 1)
    start = block * te
    expert = jnp.sum((ends[None, :] <= start[:, None]).astype(jnp.int32), axis=1)
    valid = jnp.clip(counts[expert] - (start - offsets[expert]), 0, te)
    valid = jnp.where(tile < n_used, valid, 0)
    return offsets, expert, valid, block


def _layer(x, xb, p, batch, seq, w):
    T, D = x.shape
    gu, vn, q, k, val, rs, la, sga, sgb = _inproj(xb, w["w_in"], w["w_gate_a"], w["b_gate_a"],
                                                  w["sgu_ln_g"], w["sgu_ln_b"])
    o = _gla(q, k, val, la, rs, w["gla_norm_g"], batch, seq)
    x1, x1b = _merge(x, gu, vn, o, sga, sgb, w["sgu_w"], w["sgu_b"], w["w_branch_a"], w["w_branch_b"],
                     w["w_out"], w["ln1_g"], w["ln1_b"])
    idx, rank, wt, counts = _router(x1, w["w_router"], w["b_router"])
    te = EXPERT_TILE
    n_slots = T * TOP_K + N_EXPERTS * te
    offsets, tile_expert, tile_valid, tile_block = _tile_plan(counts, n_slots // te, te)
    slot = _slots(offsets, idx, rank)
    tm = min(ROW_TILE, T)
    slot3 = slot.reshape(T // tm, 1, tm * TOP_K)
    xs = _dispatch(slot3, x1, n_slots)
    y = _experts(tile_expert, tile_valid, tile_block, xs, w["w_gate_up"], w["b_gate_up"], w["w_down"], w["b_down"])
    return _combine(slot3, x1, x1b, p, wt, w["w_ple_gate"], w["w_ple_proj"], w["ln2_g"], w["ln2_b"], y)


_WEIGHT_NAMES = ("w_in", "w_gate_a", "b_gate_a", "sgu_ln_g", "sgu_ln_b", "sgu_w", "sgu_b", "gla_norm_g",
                 "w_branch_a", "w_branch_b", "w_out", "ln1_g", "ln1_b", "w_router", "b_router", "w_gate_up",
                 "b_gate_up", "w_down", "b_down", "w_ple_proj", "w_ple_gate", "ln2_g", "ln2_b")


def kernel(x, p, w_in, w_gate_a, b_gate_a, sgu_ln_g, sgu_ln_b, sgu_w, sgu_b, gla_norm_g, w_branch_a, w_branch_b,
           w_out, ln1_g, ln1_b, w_router, b_router, w_gate_up, b_gate_up, w_down, b_down, w_ple_proj, w_ple_gate,
           ln2_g, ln2_b):
    stacked = dict(zip(_WEIGHT_NAMES, (
        w_in, w_gate_a, b_gate_a, sgu_ln_g, sgu_ln_b, sgu_w, sgu_b, gla_norm_g, w_branch_a, w_branch_b, w_out,
        ln1_g, ln1_b, w_router, b_router, w_gate_up, b_gate_up, w_down, b_down, w_ple_proj, w_ple_gate,
        ln2_g, ln2_b)))
    batch, seq, d = x.shape
    depth = w_in.shape[0]
    xf = x.reshape(batch * seq, d)
    xb = xf.astype(BF16)
    for i in range(depth):
        w = {name: arr[i] for name, arr in stacked.items()}
        xf, xb = _layer(xf, xb, p[i].reshape(batch * seq, PLE_DIM), batch, seq, w)
    return xf.reshape(batch, seq, d)
```

```python
import functools

import jax
import jax.numpy as jnp
from jax import lax
from jax.experimental import pallas as pl
from jax.experimental.pallas import tpu as pltpu

D_MODEL = 1024
DEPTH = 4
CHUNK = 64
SGU_WINDOW = 128
SGU_GROUPS = 8
GLA_HEADS = 4
GLA_DK = 128
GLA_DV = 256
GLA_GATE_RANK = 16
GLA_GATE_NORM = 16.0
N_EXPERTS = 32
TOP_K = 4
D_FF = 1024
SWIGLU_LIMIT = 7.0
SWIGLU_ALPHA = 1.702
PLE_DIM = 256
LN_EPS = 1e-5
DEEPNORM_ALPHA = (2 * DEPTH) ** 0.25
HK = GLA_HEADS * GLA_DK
HV = GLA_HEADS * GLA_DV
IN_SIZES = (D_MODEL, D_MODEL, HK, HK, HV, HV, GLA_GATE_RANK, D_MODEL, D_MODEL)

LANES = 128
ROW_SUBLANES = 8
EXPERT_LANES = 128
ROW_TILE = 256
GLA_TILE = 512
EXPERT_TILE = 256
VMEM_LIMIT = 56 * 1024 * 1024

F32 = jnp.float32
BF16 = jnp.bfloat16


def _dot(a, b, **kw):
    return jnp.dot(a, b, preferred_element_type=F32, **kw)


def _layer_norm(y, g, b):
    mu = jnp.mean(y, axis=-1, keepdims=True)
    d = y - mu
    var = jnp.mean(d * d, axis=-1, keepdims=True)
    return d * lax.rsqrt(var + LN_EPS) * g + b


def _params(*sem):
    return pltpu.CompilerParams(dimension_semantics=sem, vmem_limit_bytes=VMEM_LIMIT)


def _rows(tm, n):
    return pl.BlockSpec((tm, n), lambda i: (i, 0))


def _full(shape):
    nd = len(shape)
    return pl.BlockSpec(shape, lambda i: (0,) * nd)


def _inproj_kernel(x_ref, wu_ref, wv_ref, wq_ref, wk_ref, wval_ref, wr_ref, wglr_ref, wga_ref, wgb_ref,
                   wgate_ref, bgate_ref, lng_ref, lnb_ref,
                   gu_ref, vn_ref, q_ref, k_ref, val_ref, rs_ref, la_ref, sga_ref, sgb_ref):
    x = x_ref[...]
    gu_ref[...] = jax.nn.gelu(_dot(x, wu_ref[...])).astype(BF16)
    v = jax.nn.gelu(_dot(x, wv_ref[...]))
    vn_ref[...] = _layer_norm(v, lng_ref[...], lnb_ref[...]).astype(BF16)
    q_ref[...] = _dot(x, wq_ref[...]).astype(BF16)
    k_ref[...] = _dot(x, wk_ref[...]).astype(BF16)
    val_ref[...] = _dot(x, wval_ref[...]).astype(BF16)
    r = _dot(x, wr_ref[...])
    rs_ref[...] = (r * jax.nn.sigmoid(r)).astype(BF16)
    glr = _dot(x, wglr_ref[...])
    z = _dot(glr, wgate_ref[...], precision=lax.Precision.HIGHEST) + bgate_ref[...]
    la_ref[...] = (jnp.minimum(z, 0.0) - jnp.log(1.0 + jnp.exp(-jnp.abs(z)))) * (1.0 / GLA_GATE_NORM)
    sga_ref[...] = jax.nn.sigmoid(_dot(x, wga_ref[...])).astype(BF16)
    sgb_ref[...] = jax.nn.sigmoid(_dot(x, wgb_ref[...])).astype(BF16)


def _inproj(xb, w_in, w_gate_a, b_gate_a, ln_g, ln_b):
    T, D = xb.shape
    tm = min(ROW_TILE, T)
    splits = []
    o = 0
    for n in IN_SIZES:
        splits.append(w_in[:, o:o + n].astype(BF16))
        o += n
    wu, wv, wq, wk, wval, wr, wglr, wga, wgb = splits
    weights = [wu, wv, wq, wk, wval, wr, wglr, wga, wgb,
               w_gate_a.astype(F32), b_gate_a.reshape(1, HK).astype(F32),
               ln_g.reshape(1, D).astype(F32), ln_b.reshape(1, D).astype(F32)]
    out_shapes = [
        jax.ShapeDtypeStruct((T, D), BF16),
        jax.ShapeDtypeStruct((T, D), BF16),
        jax.ShapeDtypeStruct((T, HK), BF16),
        jax.ShapeDtypeStruct((T, HK), BF16),
        jax.ShapeDtypeStruct((T, HV), BF16),
        jax.ShapeDtypeStruct((T, HV), BF16),
        jax.ShapeDtypeStruct((T, HK), F32),
        jax.ShapeDtypeStruct((T, D), BF16),
        jax.ShapeDtypeStruct((T, D), BF16),
    ]
    return pl.pallas_call(
        _inproj_kernel,
        grid=(T // tm,),
        in_specs=[_rows(tm, D)] + [_full(w.shape) for w in weights],
        out_specs=[_rows(tm, s.shape[1]) for s in out_shapes],
        out_shape=out_shapes,
        compiler_params=_params("parallel"),
        name="inproj",
    )(xb, *weights)


def _gla_kernel(q_ref, k_ref, v_ref, la_ref, rs_ref, g_ref, o_ref, st_ref, *, n_chunks):
    @pl.when(pl.program_id(1) == 0)
    def _():
        st_ref[...] = jnp.zeros_like(st_ref)

    row = lax.broadcasted_iota(jnp.int32, (CHUNK, CHUNK), 0)
    col = lax.broadcasted_iota(jnp.int32, (CHUNK, CHUNK), 1)
    causal = row >= col
    tri = causal.astype(F32)
    nt = (((1,), (1,)), ((), ()))
    tn = (((0,), (0,)), ((), ()))

    def chunk(c, carry):
        r0 = pl.multiple_of(c * CHUNK, CHUNK)
        rows = pl.ds(r0, CHUNK)
        for h in range(GLA_HEADS):
            kc = slice(h * GLA_DK, (h + 1) * GLA_DK)
            vc = slice(h * GLA_DV, (h + 1) * GLA_DV)
            la = la_ref[rows, kc]
            b = _dot(tri, la, precision=lax.Precision.HIGHEST)
            b_end = b[CHUNK - 1:CHUNK, :]
            qh = q_ref[rows, kc].astype(F32) * (GLA_DK ** -0.5)
            kh = k_ref[rows, kc].astype(F32)
            qt = (qh * jnp.exp(b)).astype(BF16)
            kt = (kh * jnp.exp(-b)).astype(BF16)
            k_end = (kh * jnp.exp(b_end - b)).astype(BF16)
            vh = v_ref[rows, vc]
            att = lax.dot_general(qt, kt, nt, preferred_element_type=F32)
            att = jnp.where(causal, att, 0.0).astype(BF16)
            st = st_ref[h]
            o = _dot(att, vh) + lax.dot_general(qt, st.astype(BF16), nt, preferred_element_type=F32)
            st_ref[h] = jnp.exp(b_end) * st + lax.dot_general(vh, k_end, tn, preferred_element_type=F32)
            o = o * lax.rsqrt(jnp.mean(o * o, axis=-1, keepdims=True) + LN_EPS)
            o = o * g_ref[:, vc] * rs_ref[rows, vc].astype(F32)
            o_ref[rows, vc] = o.astype(BF16)
        return carry

    lax.fori_loop(0, n_chunks, chunk, 0)


def _gla(q, k, v, la, rs, norm_g, batch, seq):
    T = q.shape[0]
    ts = min(GLA_TILE, seq)
    ns = seq // ts

    def rows(n):
        return pl.BlockSpec((ts, n), lambda b, s: (b * ns + s, 0))

    return pl.pallas_call(
        functools.partial(_gla_kernel, n_chunks=ts // CHUNK),
        grid=(batch, ns),
        in_specs=[rows(HK), rows(HK), rows(HV), rows(HK), rows(HV),
                  pl.BlockSpec((1, HV), lambda b, s: (0, 0))],
        out_specs=rows(HV),
        out_shape=jax.ShapeDtypeStruct((T, HV), BF16),
        scratch_shapes=[pltpu.VMEM((GLA_HEADS, GLA_DV, GLA_DK), F32)],
        compiler_params=_params("parallel", "arbitrary"),
        name="gla",
    )(q, k, v, la, rs, norm_g.reshape(1, HV).astype(F32))


def _merge_kernel(x_ref, gu_ref, vn_ref, o_ref, sga_ref, sgb_ref, ws_ref, bs_ref, wba_ref, wbb_ref, wout_ref,
                  g_ref, b_ref, x1_ref, x1b_ref, x1r_ref, a_ref, *, n_win, tm):
    row = lax.broadcasted_iota(jnp.int32, (SGU_WINDOW, SGU_WINDOW), 0) // CHUNK
    col = lax.broadcasted_iota(jnp.int32, (SGU_WINDOW, SGU_WINDOW), 1) // CHUNK
    keep = row >= col
    gd = D_MODEL // SGU_GROUPS
    for g in range(SGU_GROUPS):
        wm = jnp.where(keep, ws_ref[g], 0.0).astype(BF16)
        bias = bs_ref[:, g:g + 1]
        cs = slice(g * gd, (g + 1) * gd)
        for w in range(n_win):
            rs = slice(w * SGU_WINDOW, (w + 1) * SGU_WINDOW)
            s = _dot(wm, vn_ref[rs, cs]) + bias
            a_ref[rs, cs] = (gu_ref[rs, cs].astype(F32) * s).astype(BF16)
    h = (sga_ref[...].astype(F32) * _dot(a_ref[...], wba_ref[...])
         + sgb_ref[...].astype(F32) * _dot(o_ref[...], wbb_ref[...]))
    mix = _dot(h.astype(BF16), wout_ref[...])
    x1 = _layer_norm(DEEPNORM_ALPHA * x_ref[...] + mix, g_ref[...], b_ref[...])
    x1_ref[...] = x1
    x1b_ref[...] = x1.astype(BF16)
    for s in range(ROW_SUBLANES):
        x1r_ref[pl.ds(s, tm, stride=ROW_SUBLANES), :] = x1[:, s * LANES:(s + 1) * LANES]


def _merge(x, gu, vn, o, sga, sgb, sgu_w, sgu_b, w_ba, w_bb, w_out, ln_g, ln_b):
    T, D = x.shape
    tm = min(ROW_TILE, T)
    weights = [sgu_w.astype(F32), sgu_b.T.astype(F32), w_ba.astype(BF16), w_bb.astype(BF16), w_out.astype(BF16),
               ln_g.reshape(1, D).astype(F32), ln_b.reshape(1, D).astype(F32)]
    return pl.pallas_call(
        functools.partial(_merge_kernel, n_win=tm // SGU_WINDOW, tm=tm),
        grid=(T // tm,),
        in_specs=[_rows(tm, D)] * 6 + [_full(w.shape) for w in weights],
        out_specs=[_rows(tm, D), _rows(tm, D), _rows(tm * ROW_SUBLANES, LANES)],
        out_shape=[jax.ShapeDtypeStruct((T, D), F32), jax.ShapeDtypeStruct((T, D), BF16),
                   jax.ShapeDtypeStruct((T * ROW_SUBLANES, LANES), F32)],
        scratch_shapes=[pltpu.VMEM((tm, D), BF16)],
        compiler_params=_params("parallel"),
        name="merge",
    )(x, gu, vn, o, sga, sgb, *weights)


def _router_kernel(x_ref, wr_ref, br_ref, wt_ref, cnt_ref, pre_ref, code_ref, carry_ref, *, tm):
    @pl.when(pl.program_id(0) == 0)
    def _():
        carry_ref[...] = jnp.zeros_like(carry_ref)

    logits = _dot(x_ref[...], wr_ref[...], precision=lax.Precision.HIGHEST) + br_ref[...]
    lane = lax.broadcasted_iota(jnp.int32, (tm, EXPERT_LANES), 1)
    work = logits
    hots, vals = [], []
    for _ in range(TOP_K):
        m = jnp.max(work, axis=-1, keepdims=True)
        first = jnp.min(jnp.where(work == m, lane, EXPERT_LANES), axis=-1, keepdims=True)
        hot = lane == first
        hots.append(hot)
        vals.append(m)
        work = jnp.where(hot, -jnp.inf, work)
    exps = [jnp.exp(v - vals[0]) for v in vals]
    denom = exps[0] + exps[1] + exps[2] + exps[3]
    for j in range(TOP_K):
        wt_ref[:, j:j + 1] = exps[j] / denom

    chosen = hots[0] | hots[1] | hots[2] | hots[3]
    mask = chosen.astype(F32)
    which = hots[1].astype(F32) + 2.0 * hots[2].astype(F32) + 3.0 * hots[3].astype(F32)
    r = lax.broadcasted_iota(jnp.int32, (tm, tm), 0)
    c = lax.broadcasted_iota(jnp.int32, (tm, tm), 1)
    below = (r > c).astype(BF16)
    rank = _dot(below, mask.astype(BF16)) + carry_ref[...]
    pre_ref[0] = carry_ref[...]
    carry_ref[...] += jnp.sum(mask, axis=0, keepdims=True)
    cnt_ref[...] = carry_ref[...]
    code = jnp.where(chosen, TOP_K * rank + which, -float(TOP_K))
    code_ref[0] = code.T[:N_EXPERTS, :].astype(jnp.int32)


def _router(x1, w_router, b_router):
    T, D = x1.shape
    tm = min(ROW_TILE, T)
    n_tt = T // tm
    pad = EXPERT_LANES - N_EXPERTS
    wr = jnp.pad(w_router.astype(F32), ((0, 0), (0, pad)))
    br = jnp.pad(b_router.reshape(1, N_EXPERTS).astype(F32), ((0, 0), (0, pad)), constant_values=-1e30)
    return pl.pallas_call(
        functools.partial(_router_kernel, tm=tm),
        grid=(n_tt,),
        in_specs=[_rows(tm, D), _full((D, EXPERT_LANES)), _full((1, EXPERT_LANES))],
        out_specs=[_rows(tm, TOP_K), _full((1, EXPERT_LANES)),
                   pl.BlockSpec((1, 1, EXPERT_LANES), lambda i: (i, 0, 0)),
                   pl.BlockSpec((1, N_EXPERTS, tm), lambda i: (i, 0, 0))],
        out_shape=[jax.ShapeDtypeStruct((T, TOP_K), F32), jax.ShapeDtypeStruct((1, EXPERT_LANES), F32),
                   jax.ShapeDtypeStruct((n_tt, 1, EXPERT_LANES), F32),
                   jax.ShapeDtypeStruct((n_tt, N_EXPERTS, tm), jnp.int32)],
        scratch_shapes=[pltpu.VMEM((1, EXPERT_LANES), F32)],
        compiler_params=_params("arbitrary"),
        name="router",
    )(x1, wr, br)


def _tile_plan(counts, prefix, n_tiles, te):
    counts = counts.astype(jnp.int32)
    prefix = prefix.astype(jnp.int32)
    padded = ((counts + te - 1) // te) * te
    ends = jnp.cumsum(padded)
    offsets = ends - padded
    n_used = ends[-1] // te
    tile = jnp.arange(n_tiles, dtype=jnp.int32)
    used = tile < n_used
    start = jnp.minimum(tile, n_used - 1) * te
    expert = jnp.sum((ends[None, :] <= start[:, None]).astype(jnp.int32), axis=1)
    rank0 = start - offsets[expert]
    before = prefix[:, expert].T
    after = jnp.concatenate([prefix[1:], counts[None, :]], axis=0)[:, expert].T
    lo = jnp.sum((after <= rank0[:, None]).astype(jnp.int32), axis=1)
    hi = jnp.sum((before < (rank0 + te)[:, None]).astype(jnp.int32), axis=1)
    lo = jnp.where(used, lo, 0)
    hi = jnp.where(used, hi, 0)
    return expert, rank0, lo, hi


def _invmap_kernel(te_ref, r0_ref, lo_ref, hi_ref, code_ref, gid_ref, sid_ref, *, tm, te, n_tokens):
    j = pl.program_id(0)
    e = te_ref[j]
    want = r0_ref[j] + lax.broadcasted_iota(jnp.int32, (te, 1), 0)
    sub = lax.broadcasted_iota(jnp.int32, (ROW_SUBLANES, tm), 0)
    tloc = lax.broadcasted_iota(jnp.int32, (ROW_SUBLANES, tm), 1).astype(F32)
    nt = (((1,), (1,)), ((), ()))

    def body(i, acc):
        cnt, tok, kk = acc
        code = code_ref[i, pl.ds(e, 1), :]
        hit = ((code >> 2) == want).astype(F32).astype(BF16)
        which = (code & (TOP_K - 1)).astype(F32)
        sel = jnp.where(sub == 0, 1.0, jnp.where(sub == 1, tloc, jnp.where(sub == 2, which, 0.0))).astype(BF16)
        res = lax.dot_general(sel, hit, nt, preferred_element_type=F32)
        base = (i * tm).astype(F32)
        return cnt + res[0:1], tok + res[1:2] + base * res[0:1], kk + res[2:3]

    zero = jnp.zeros((1, te), F32)
    cnt, tok, kk = lax.fori_loop(lo_ref[j], hi_ref[j], body, (zero, zero, zero))
    valid = cnt > 0.5
    spare = n_tokens * TOP_K + lax.broadcasted_iota(jnp.int32, (1, te), 1)
    gid_ref[0] = jnp.where(valid, tok, 0.0).astype(jnp.int32)
    sid_ref[0] = jnp.where(valid, (kk * n_tokens + tok).astype(jnp.int32), spare)


def _invmap(tile_expert, rank0, lo, hi, code, n_tiles, te, n_tokens):
    n_tt, _, tm = code.shape
    spec = pl.BlockSpec((1, 1, te), lambda j, *_: (j, 0, 0))
    return pl.pallas_call(
        functools.partial(_invmap_kernel, tm=tm, te=te, n_tokens=n_tokens),
        grid_spec=pltpu.PrefetchScalarGridSpec(
            num_scalar_prefetch=4, grid=(n_tiles,),
            in_specs=[pl.BlockSpec((n_tt, N_EXPERTS, tm), lambda j, *_: (0, 0, 0))],
            out_specs=[spec, spec]),
        out_shape=[jax.ShapeDtypeStruct((n_tiles, 1, te), jnp.int32)] * 2,
        compiler_params=_params("parallel"),
        name="invmap",
    )(tile_expert, rank0, lo, hi, code)


def _moe_kernel(te_ref, gid0_ref, gidn_ref, sidp_ref, sidc_ref, x1r_ref, wgu_ref, bgu_ref, wd_ref, bd_ref, y4_ref,
                xg, yb, gsem, ssem, *, te, n_tiles):
    j = pl.program_id(0)
    slot = j % 2
    other = 1 - slot
    rows = te * ROW_SUBLANES

    def gather_row(ids_ref, par, r):
        src = pl.multiple_of(ids_ref[0, 0, r] * ROW_SUBLANES, ROW_SUBLANES)
        return pltpu.make_async_copy(x1r_ref.at[pl.ds(src, ROW_SUBLANES), :],
                                     xg.at[par, pl.ds(r * ROW_SUBLANES, ROW_SUBLANES), :], gsem.at[par])

    def scatter_row(ids_ref, par, r):
        dst = pl.multiple_of(ids_ref[0, 0, r] * ROW_SUBLANES, ROW_SUBLANES)
        return pltpu.make_async_copy(yb.at[par, pl.ds(r * ROW_SUBLANES, ROW_SUBLANES), :],
                                     y4_ref.at[pl.ds(dst, ROW_SUBLANES), :], ssem.at[par])

    def wait_gather(par):
        pltpu.make_async_copy(x1r_ref.at[pl.ds(0, rows), :], xg.at[par], gsem.at[par]).wait()

    def wait_scatter(par):
        pltpu.make_async_copy(yb.at[par], y4_ref.at[pl.ds(0, rows), :], ssem.at[par]).wait()

    @pl.when(j == 0)
    def _():
        yb[...] = jnp.zeros_like(yb)

        def first(r, c):
            gather_row(gid0_ref, 0, r).start()
            return c

        lax.fori_loop(0, te, first, 0)

    @pl.when(j > 0)
    def _():
        wait_scatter(slot)

    wait_gather(slot)

    x = jnp.concatenate([xg[slot, pl.ds(s, te, stride=ROW_SUBLANES), :].astype(BF16)
                         for s in range(ROW_SUBLANES)], axis=1)
    for r in range(te):
        gather_row(gidn_ref, other, r).start()
    for r in range(te):
        scatter_row(sidp_ref, other, r).start()
    hg = _dot(x, wgu_ref[:, :D_FF]) + bgu_ref[:, :D_FF]
    runtime_zero = jnp.minimum(te_ref[j], 0)
    probe = pltpu.bitcast(xg[slot + runtime_zero, 0:ROW_SUBLANES, :], jnp.int32) & runtime_zero
    plus_zero = pltpu.bitcast(probe, F32)[0:1, :]
    bias_up = bgu_ref[:, D_FF:] + jnp.concatenate([plus_zero] * (D_FF // LANES), axis=1)
    hu = _dot(x, wgu_ref[:, D_FF:]) + bias_up
    gate = jnp.minimum(hg, SWIGLU_LIMIT)
    up = jnp.clip(hu, -SWIGLU_LIMIT, SWIGLU_LIMIT)
    act = (up + 1.0) * gate * jax.nn.sigmoid(SWIGLU_ALPHA * gate)
    y = _dot(act.astype(BF16), wd_ref[...]) + bd_ref[...]
    for s in range(ROW_SUBLANES):
        yb[slot, pl.ds(s, te, stride=ROW_SUBLANES), :] = y[:, s * LANES:(s + 1) * LANES]

    @pl.when(j == n_tiles - 1)
    def _():
        wait_scatter(other)
        wait_gather(other)

        def last(r, c):
            scatter_row(sidc_ref, slot, r).start()
            return c

        lax.fori_loop(0, te, last, 0)
        wait_scatter(slot)


def _moe(tile_expert, gid, sid, x1r, w_gate_up, b_gate_up, w_down, b_down, n_tokens):
    n_tiles, _, te = gid.shape
    D = D_MODEL
    spare = (n_tokens * TOP_K + jnp.arange(te, dtype=jnp.int32)).reshape(1, 1, te)
    sid_prev = jnp.concatenate([spare, sid], axis=0)
    last = n_tiles - 1

    def ids(index):
        return pl.BlockSpec((1, 1, te), index, memory_space=pltpu.SMEM)

    return pl.pallas_call(
        functools.partial(_moe_kernel, te=te, n_tiles=n_tiles),
        grid_spec=pltpu.PrefetchScalarGridSpec(
            num_scalar_prefetch=1,
            grid=(n_tiles,),
            in_specs=[
                ids(lambda j, e: (0, 0, 0)),
                ids(lambda j, e: (jnp.minimum(j + 1, last), 0, 0)),
                ids(lambda j, e: (j, 0, 0)),
                ids(lambda j, e: (j + 1, 0, 0)),
                pl.BlockSpec(memory_space=pl.ANY),
                pl.BlockSpec((None, D, 2 * D_FF), lambda j, e: (e[j], 0, 0)),
                pl.BlockSpec((None, 1, 2 * D_FF), lambda j, e: (e[j], 0, 0)),
                pl.BlockSpec((None, D_FF, D), lambda j, e: (e[j], 0, 0)),
                pl.BlockSpec((None, 1, D), lambda j, e: (e[j], 0, 0)),
            ],
            out_specs=pl.BlockSpec(memory_space=pl.ANY),
            scratch_shapes=[pltpu.VMEM((2, te * ROW_SUBLANES, LANES), F32),
                            pltpu.VMEM((2, te * ROW_SUBLANES, LANES), F32),
                            pltpu.SemaphoreType.DMA((2,)), pltpu.SemaphoreType.DMA((2,))],
        ),
        out_shape=jax.ShapeDtypeStruct(((n_tokens * TOP_K + te) * ROW_SUBLANES, LANES), F32),
        compiler_params=_params("arbitrary"),
        name="moe",
    )(tile_expert, gid, gid, sid_prev, sid_prev, x1r,
      w_gate_up.astype(BF16), b_gate_up.reshape(N_EXPERTS, 1, 2 * D_FF).astype(F32),
      w_down.astype(BF16), b_down.reshape(N_EXPERTS, 1, D).astype(F32))


def _combine_kernel(x1_ref, x1b_ref, p_ref, wt_ref, wpg_ref, wpp_ref, g_ref, b_ref, y0_ref, y1_ref, y2_ref, y3_ref,
                    x2_ref, x2b_ref, *, tm):
    ple = jax.nn.sigmoid(_dot(x1b_ref[...], wpg_ref[...])) * _dot(p_ref[...].astype(BF16), wpp_ref[...])
    pieces = []
    for s in range(ROW_SUBLANES):
        acc = None
        for j, y_ref in enumerate((y0_ref, y1_ref, y2_ref, y3_ref)):
            term = wt_ref[:, j:j + 1] * y_ref[pl.ds(s, tm, stride=ROW_SUBLANES), :]
            acc = term if acc is None else acc + term
        pieces.append(acc)
    ffn = jnp.concatenate(pieces, axis=1)
    x2 = _layer_norm(DEEPNORM_ALPHA * x1_ref[...] + ffn + ple, g_ref[...], b_ref[...])
    x2_ref[...] = x2
    x2b_ref[...] = x2.astype(BF16)


def _combine(x1, x1b, p, wt, w_ple_gate, w_ple_proj, ln_g, ln_b, y4):
    T, D = x1.shape
    tm = min(ROW_TILE, T)
    n_tt = T // tm
    weights = [w_ple_gate.astype(BF16), w_ple_proj.astype(BF16),
               ln_g.reshape(1, D).astype(F32), ln_b.reshape(1, D).astype(F32)]

    def choice(j):
        return pl.BlockSpec((tm * ROW_SUBLANES, LANES), lambda i: (j * n_tt + i, 0))

    return pl.pallas_call(
        functools.partial(_combine_kernel, tm=tm),
        grid=(n_tt,),
        in_specs=[_rows(tm, D), _rows(tm, D), _rows(tm, PLE_DIM), _rows(tm, TOP_K)]
                 + [_full(w.shape) for w in weights] + [choice(j) for j in range(TOP_K)],
        out_specs=[_rows(tm, D), _rows(tm, D)],
        out_shape=[jax.ShapeDtypeStruct((T, D), F32), jax.ShapeDtypeStruct((T, D), BF16)],
        compiler_params=_params("parallel"),
        name="combine",
    )(x1, x1b, p, wt, *weights, y4, y4, y4, y4)


def _layer(x, xb, p, batch, seq, w):
    T, D = x.shape
    gu, vn, q, k, val, rs, la, sga, sgb = _inproj(xb, w["w_in"], w["w_gate_a"], w["b_gate_a"],
                                                  w["sgu_ln_g"], w["sgu_ln_b"])
    o = _gla(q, k, val, la, rs, w["gla_norm_g"], batch, seq)
    x1, x1b, x1r = _merge(x, gu, vn, o, sga, sgb, w["sgu_w"], w["sgu_b"], w["w_branch_a"], w["w_branch_b"],
                          w["w_out"], w["ln1_g"], w["ln1_b"])
    wt, counts, prefix, code = _router(x1, w["w_router"], w["b_router"])
    te = EXPERT_TILE
    n_tiles = (T * TOP_K) // te + N_EXPERTS
    tile_expert, rank0, lo, hi = _tile_plan(counts[0, :N_EXPERTS], prefix[:, 0, :N_EXPERTS], n_tiles, te)
    gid, sid = _invmap(tile_expert, rank0, lo, hi, code, n_tiles, te, T)
    y4 = _moe(tile_expert, gid, sid, x1r, w["w_gate_up"], w["b_gate_up"], w["w_down"], w["b_down"], T)
    return _combine(x1, x1b, p, wt, w["w_ple_gate"], w["w_ple_proj"], w["ln2_g"], w["ln2_b"], y4)


_WEIGHT_NAMES = ("w_in", "w_gate_a", "b_gate_a", "sgu_ln_g", "sgu_ln_b", "sgu_w", "sgu_b", "gla_norm_g",
                 "w_branch_a", "w_branch_b", "w_out", "ln1_g", "ln1_b", "w_router", "b_router", "w_gate_up",
                 "b_gate_up", "w_down", "b_down", "w_ple_proj", "w_ple_gate", "ln2_g", "ln2_b")


def kernel(x, p, w_in, w_gate_a, b_gate_a, sgu_ln_g, sgu_ln_b, sgu_w, sgu_b, gla_norm_g, w_branch_a, w_branch_b,
           w_out, ln1_g, ln1_b, w_router, b_router, w_gate_up, b_gate_up, w_down, b_down, w_ple_proj, w_ple_gate,
           ln2_g, ln2_b):
    stacked = dict(zip(_WEIGHT_NAMES, (
        w_in, w_gate_a, b_gate_a, sgu_ln_g, sgu_ln_b, sgu_w, sgu_b, gla_norm_g, w_branch_a, w_branch_b, w_out,
        ln1_g, ln1_b, w_router, b_router, w_gate_up, b_gate_up, w_down, b_down, w_ple_proj, w_ple_gate,
        ln2_g, ln2_b)))
    batch, seq, d = x.shape
    depth = w_in.shape[0]
    xf = x.reshape(batch * seq, d)
    xb = xf.astype(BF16)
    for i in range(depth):
        w = {name: arr[i] for name, arr in stacked.items()}
        xf, xb = _layer(xf, xb, p[i].reshape(batch * seq, PLE_DIM), batch, seq, w)
    return xf.reshape(batch, seq, d)
```

```python
import functools

import jax
import jax.numpy as jnp
from jax import lax
from jax.experimental import pallas as pl
from jax.experimental.pallas import tpu as pltpu

D_MODEL = 1024
DEPTH = 4
CHUNK = 64
SGU_WINDOW = 128
SGU_GROUPS = 8
GLA_HEADS = 4
GLA_DK = 128
GLA_DV = 256
GLA_GATE_RANK = 16
GLA_GATE_NORM = 16.0
N_EXPERTS = 32
TOP_K = 4
D_FF = 1024
SWIGLU_LIMIT = 7.0
SWIGLU_ALPHA = 1.702
PLE_DIM = 256
LN_EPS = 1e-5
DEEPNORM_ALPHA = (2 * DEPTH) ** 0.25
HK = GLA_HEADS * GLA_DK
HV = GLA_HEADS * GLA_DV
IN_SIZES = (D_MODEL, D_MODEL, HK, HK, HV, HV, GLA_GATE_RANK, D_MODEL, D_MODEL)

LANES = 128
ROW_SUBLANES = 8
EXPERT_LANES = 128
ROW_TILE = 256
INPROJ_TILE = 512
GLA_TILE = 512
EXPERT_TILE = 256
DMA_PRIORITIES = 2
INVMAP_UNROLL = 4
VMEM_LIMIT = 56 * 1024 * 1024

F32 = jnp.float32
BF16 = jnp.bfloat16


def _dot(a, b, **kw):
    return jnp.dot(a, b, preferred_element_type=F32, **kw)


def _layer_norm(y, g, b):
    mu = jnp.mean(y, axis=-1, keepdims=True)
    d = y - mu
    var = jnp.mean(d * d, axis=-1, keepdims=True)
    return d * lax.rsqrt(var + LN_EPS) * g + b


def _params(*sem):
    return pltpu.CompilerParams(dimension_semantics=sem, vmem_limit_bytes=VMEM_LIMIT)


def _rows(tm, n):
    return pl.BlockSpec((tm, n), lambda i: (i, 0))


def _full(shape):
    nd = len(shape)
    return pl.BlockSpec(shape, lambda i: (0,) * nd)


def _resident(shape):
    nd = len(shape)
    return pl.BlockSpec(shape, lambda i: (0,) * nd, pipeline_mode=pl.Buffered(1))


def _inproj_kernel(x_ref, wu_ref, wv_ref, wq_ref, wk_ref, wval_ref, wr_ref, wglr_ref, wga_ref, wgb_ref,
                   wgate_ref, bgate_ref, lng_ref, lnb_ref,
                   gu_ref, vn_ref, q_ref, k_ref, val_ref, rs_ref, la_ref, sga_ref, sgb_ref):
    x = x_ref[...]
    gu_ref[...] = jax.nn.gelu(_dot(x, wu_ref[...])).astype(BF16)
    v = jax.nn.gelu(_dot(x, wv_ref[...]))
    vn_ref[...] = _layer_norm(v, lng_ref[...], lnb_ref[...]).astype(BF16)
    q_ref[...] = _dot(x, wq_ref[...]).astype(BF16)
    k_ref[...] = _dot(x, wk_ref[...]).astype(BF16)
    val_ref[...] = _dot(x, wval_ref[...]).astype(BF16)
    r = _dot(x, wr_ref[...])
    rs_ref[...] = (r * jax.nn.sigmoid(r)).astype(BF16)
    glr = _dot(x, wglr_ref[...])
    z = _dot(glr, wgate_ref[...], precision=lax.Precision.HIGHEST) + bgate_ref[...]
    la_ref[...] = (jnp.minimum(z, 0.0) - jnp.log(1.0 + jnp.exp(-jnp.abs(z)))) * (1.0 / GLA_GATE_NORM)
    sga_ref[...] = jax.nn.sigmoid(_dot(x, wga_ref[...])).astype(BF16)
    sgb_ref[...] = jax.nn.sigmoid(_dot(x, wgb_ref[...])).astype(BF16)


def _inproj(xb, w_in, w_gate_a, b_gate_a, ln_g, ln_b):
    T, D = xb.shape
    tm = min(INPROJ_TILE, T)
    splits = []
    o = 0
    for n in IN_SIZES:
        splits.append(w_in[:, o:o + n].astype(BF16))
        o += n
    wu, wv, wq, wk, wval, wr, wglr, wga, wgb = splits
    weights = [wu, wv, wq, wk, wval, wr, wglr, wga, wgb,
               w_gate_a.astype(F32), b_gate_a.reshape(1, HK).astype(F32),
               ln_g.reshape(1, D).astype(F32), ln_b.reshape(1, D).astype(F32)]
    out_shapes = [
        jax.ShapeDtypeStruct((T, D), BF16),
        jax.ShapeDtypeStruct((T, D), BF16),
        jax.ShapeDtypeStruct((T, HK), BF16),
        jax.ShapeDtypeStruct((T, HK), BF16),
        jax.ShapeDtypeStruct((T, HV), BF16),
        jax.ShapeDtypeStruct((T, HV), BF16),
        jax.ShapeDtypeStruct((T, HK), F32),
        jax.ShapeDtypeStruct((T, D), BF16),
        jax.ShapeDtypeStruct((T, D), BF16),
    ]
    return pl.pallas_call(
        _inproj_kernel,
        grid=(T // tm,),
        in_specs=[_rows(tm, D)] + [_resident(w.shape) for w in weights],
        out_specs=[_rows(tm, s.shape[1]) for s in out_shapes],
        out_shape=out_shapes,
        compiler_params=_params("parallel"),
        name="inproj",
    )(xb, *weights)


def _gla_kernel(q_ref, k_ref, v_ref, la_ref, rs_ref, g_ref, o_ref, st_ref, *, n_chunks):
    @pl.when(pl.program_id(1) == 0)
    def _():
        st_ref[...] = jnp.zeros_like(st_ref)

    row = lax.broadcasted_iota(jnp.int32, (CHUNK, CHUNK), 0)
    col = lax.broadcasted_iota(jnp.int32, (CHUNK, CHUNK), 1)
    causal = row >= col
    tri = causal.astype(F32)
    nt = (((1,), (1,)), ((), ()))
    tn = (((0,), (0,)), ((), ()))

    def chunk(c, carry):
        r0 = pl.multiple_of(c * CHUNK, CHUNK)
        rows = pl.ds(r0, CHUNK)
        for h in range(GLA_HEADS):
            kc = slice(h * GLA_DK, (h + 1) * GLA_DK)
            vc = slice(h * GLA_DV, (h + 1) * GLA_DV)
            la = la_ref[rows, kc]
            b = _dot(tri, la, precision=lax.Precision.HIGHEST)
            b_end = b[CHUNK - 1:CHUNK, :]
            qh = q_ref[rows, kc].astype(F32) * (GLA_DK ** -0.5)
            kh = k_ref[rows, kc].astype(F32)
            qt = (qh * jnp.exp(b)).astype(BF16)
            kt = (kh * jnp.exp(-b)).astype(BF16)
            k_end = (kh * jnp.exp(b_end - b)).astype(BF16)
            vh = v_ref[rows, vc]
            att = lax.dot_general(qt, kt, nt, preferred_element_type=F32)
            att = jnp.where(causal, att, 0.0).astype(BF16)
            st = st_ref[h]
            o = _dot(att, vh) + lax.dot_general(qt, st.astype(BF16), nt, preferred_element_type=F32)
            st_ref[h] = jnp.exp(b_end) * st + lax.dot_general(vh, k_end, tn, preferred_element_type=F32)
            o = o * lax.rsqrt(jnp.mean(o * o, axis=-1, keepdims=True) + LN_EPS)
            o = o * g_ref[:, vc] * rs_ref[rows, vc].astype(F32)
            o_ref[rows, vc] = o.astype(BF16)
        return carry

    lax.fori_loop(0, n_chunks, chunk, 0, unroll=True)


def _gla(q, k, v, la, rs, norm_g, batch, seq):
    T = q.shape[0]
    ts = min(GLA_TILE, seq)
    ns = seq // ts

    def rows(n):
        return pl.BlockSpec((ts, n), lambda b, s: (b * ns + s, 0))

    return pl.pallas_call(
        functools.partial(_gla_kernel, n_chunks=ts // CHUNK),
        grid=(batch, ns),
        in_specs=[rows(HK), rows(HK), rows(HV), rows(HK), rows(HV),
                  pl.BlockSpec((1, HV), lambda b, s: (0, 0))],
        out_specs=rows(HV),
        out_shape=jax.ShapeDtypeStruct((T, HV), BF16),
        scratch_shapes=[pltpu.VMEM((GLA_HEADS, GLA_DV, GLA_DK), F32)],
        compiler_params=_params("parallel", "arbitrary"),
        name="gla",
    )(q, k, v, la, rs, norm_g.reshape(1, HV).astype(F32))


def _merge_kernel(x_ref, gu_ref, vn_ref, o_ref, sga_ref, sgb_ref, ws_ref, bs_ref, wba_ref, wbb_ref, wout_ref,
                  g_ref, b_ref, x1_ref, x1b_ref, x1r_ref, a_ref, *, n_win, tm):
    row = lax.broadcasted_iota(jnp.int32, (SGU_WINDOW, SGU_WINDOW), 0) // CHUNK
    col = lax.broadcasted_iota(jnp.int32, (SGU_WINDOW, SGU_WINDOW), 1) // CHUNK
    keep = row >= col
    gd = D_MODEL // SGU_GROUPS
    for g in range(SGU_GROUPS):
        wm = jnp.where(keep, ws_ref[g], 0.0).astype(BF16)
        bias = bs_ref[:, g:g + 1]
        cs = slice(g * gd, (g + 1) * gd)
        for w in range(n_win):
            rs = slice(w * SGU_WINDOW, (w + 1) * SGU_WINDOW)
            s = _dot(wm, vn_ref[rs, cs]) + bias
            a_ref[rs, cs] = (gu_ref[rs, cs].astype(F32) * s).astype(BF16)
    h = (sga_ref[...].astype(F32) * _dot(a_ref[...], wba_ref[...])
         + sgb_ref[...].astype(F32) * _dot(o_ref[...], wbb_ref[...]))
    mix = _dot(h.astype(BF16), wout_ref[...])
    x1 = _layer_norm(DEEPNORM_ALPHA * x_ref[...] + mix, g_ref[...], b_ref[...])
    x1_ref[...] = x1
    x1b_ref[...] = x1.astype(BF16)
    for s in range(ROW_SUBLANES):
        x1r_ref[pl.ds(s, tm, stride=ROW_SUBLANES), :] = x1[:, s * LANES:(s + 1) * LANES]


def _merge(x, gu, vn, o, sga, sgb, sgu_w, sgu_b, w_ba, w_bb, w_out, ln_g, ln_b):
    T, D = x.shape
    tm = min(ROW_TILE, T)
    weights = [sgu_w.astype(F32), sgu_b.T.astype(F32), w_ba.astype(BF16), w_bb.astype(BF16), w_out.astype(BF16),
               ln_g.reshape(1, D).astype(F32), ln_b.reshape(1, D).astype(F32)]
    return pl.pallas_call(
        functools.partial(_merge_kernel, n_win=tm // SGU_WINDOW, tm=tm),
        grid=(T // tm,),
        in_specs=[_rows(tm, D)] * 6 + [_full(w.shape) for w in weights],
        out_specs=[_rows(tm, D), _rows(tm, D), _rows(tm * ROW_SUBLANES, LANES)],
        out_shape=[jax.ShapeDtypeStruct((T, D), F32), jax.ShapeDtypeStruct((T, D), BF16),
                   jax.ShapeDtypeStruct((T * ROW_SUBLANES, LANES), F32)],
        scratch_shapes=[pltpu.VMEM((tm, D), BF16)],
        compiler_params=_params("parallel"),
        name="merge",
    )(x, gu, vn, o, sga, sgb, *weights)


def _router_kernel(x_ref, wr_ref, br_ref, wt_ref, cnt_ref, pre_ref, code_ref, carry_ref, *, tm):
    @pl.when(pl.program_id(0) == 0)
    def _():
        carry_ref[...] = jnp.zeros_like(carry_ref)

    logits = _dot(x_ref[...], wr_ref[...], precision=lax.Precision.HIGHEST) + br_ref[...]
    lane = lax.broadcasted_iota(jnp.int32, (tm, EXPERT_LANES), 1)
    work = logits
    hots, vals = [], []
    for _ in range(TOP_K):
        m = jnp.max(work, axis=-1, keepdims=True)
        first = jnp.min(jnp.where(work == m, lane, EXPERT_LANES), axis=-1, keepdims=True)
        hot = lane == first
        hots.append(hot)
        vals.append(m)
        work = jnp.where(hot, -jnp.inf, work)
    exps = [jnp.exp(v - vals[0]) for v in vals]
    denom = exps[0] + exps[1] + exps[2] + exps[3]
    for j in range(TOP_K):
        wt_ref[:, j:j + 1] = exps[j] / denom

    chosen = hots[0] | hots[1] | hots[2] | hots[3]
    mask = chosen.astype(F32)
    which = hots[1].astype(F32) + 2.0 * hots[2].astype(F32) + 3.0 * hots[3].astype(F32)
    r = lax.broadcasted_iota(jnp.int32, (tm, tm), 0)
    c = lax.broadcasted_iota(jnp.int32, (tm, tm), 1)
    below = (r > c).astype(BF16)
    rank = _dot(below, mask.astype(BF16)) + carry_ref[...]
    pre_ref[0] = carry_ref[...]
    carry_ref[...] += jnp.sum(mask, axis=0, keepdims=True)
    cnt_ref[...] = carry_ref[...]
    code = jnp.where(chosen, TOP_K * rank + which, -float(TOP_K))
    code_ref[0] = code.T[:N_EXPERTS, :].astype(jnp.int32)


def _router(x1, w_router, b_router):
    T, D = x1.shape
    tm = min(ROW_TILE, T)
    n_tt = T // tm
    pad = EXPERT_LANES - N_EXPERTS
    wr = jnp.pad(w_router.astype(F32), ((0, 0), (0, pad)))
    br = jnp.pad(b_router.reshape(1, N_EXPERTS).astype(F32), ((0, 0), (0, pad)), constant_values=-1e30)
    return pl.pallas_call(
        functools.partial(_router_kernel, tm=tm),
        grid=(n_tt,),
        in_specs=[_rows(tm, D), _full((D, EXPERT_LANES)), _full((1, EXPERT_LANES))],
        out_specs=[_rows(tm, TOP_K), _full((1, EXPERT_LANES)),
                   pl.BlockSpec((1, 1, EXPERT_LANES), lambda i: (i, 0, 0)),
                   pl.BlockSpec((1, N_EXPERTS, tm), lambda i: (i, 0, 0))],
        out_shape=[jax.ShapeDtypeStruct((T, TOP_K), F32), jax.ShapeDtypeStruct((1, EXPERT_LANES), F32),
                   jax.ShapeDtypeStruct((n_tt, 1, EXPERT_LANES), F32),
                   jax.ShapeDtypeStruct((n_tt, N_EXPERTS, tm), jnp.int32)],
        scratch_shapes=[pltpu.VMEM((1, EXPERT_LANES), F32)],
        compiler_params=_params("arbitrary"),
        name="router",
    )(x1, wr, br)


def _tile_plan(counts, prefix, n_tiles, te):
    counts = counts.astype(jnp.int32)
    prefix = prefix.astype(jnp.int32)
    padded = ((counts + te - 1) // te) * te
    ends = jnp.cumsum(padded)
    offsets = ends - padded
    n_used = ends[-1] // te
    tile = jnp.arange(n_tiles, dtype=jnp.int32)
    used = tile < n_used
    start = jnp.minimum(tile, n_used - 1) * te
    expert = jnp.sum((ends[None, :] <= start[:, None]).astype(jnp.int32), axis=1)
    rank0 = start - offsets[expert]
    before = prefix[:, expert].T
    after = jnp.concatenate([prefix[1:], counts[None, :]], axis=0)[:, expert].T
    lo = jnp.sum((after <= rank0[:, None]).astype(jnp.int32), axis=1)
    hi = jnp.sum((before < (rank0 + te)[:, None]).astype(jnp.int32), axis=1)
    lo = jnp.where(used, lo, 0)
    hi = jnp.where(used, hi, 0)
    return expert, rank0, lo, hi


def _invmap_kernel(te_ref, r0_ref, lo_ref, hi_ref, code_ref, gid_ref, sid_ref, *, tm, te, n_tokens):
    j = pl.program_id(0)
    e = te_ref[j]
    want = r0_ref[j] + lax.broadcasted_iota(jnp.int32, (te, 1), 0)
    sub = lax.broadcasted_iota(jnp.int32, (ROW_SUBLANES, tm), 0)
    tloc = lax.broadcasted_iota(jnp.int32, (ROW_SUBLANES, tm), 1).astype(F32)
    nt = (((1,), (1,)), ((), ()))

    lo = lo_ref[j]
    hi = hi_ref[j]
    n_tt = code_ref.shape[0]

    def body(g, acc):
        cnt, tok, kk = acc
        for u in range(INVMAP_UNROLL):
            i = lo + g * INVMAP_UNROLL + u
            ic = jnp.minimum(i, n_tt - 1)
            code = code_ref[ic, pl.ds(e, 1), :]
            rank = jnp.where(i < hi, code >> 2, -1)
            hit = (rank == want).astype(F32).astype(BF16)
            which = (code & (TOP_K - 1)).astype(F32)
            sel = jnp.where(sub == 0, 1.0, jnp.where(sub == 1, tloc, jnp.where(sub == 2, which, 0.0))).astype(BF16)
            res = lax.dot_general(sel, hit, nt, preferred_element_type=F32)
            base = (ic * tm).astype(F32)
            cnt, tok, kk = cnt + res[0:1], tok + res[1:2] + base * res[0:1], kk + res[2:3]
        return cnt, tok, kk

    zero = jnp.zeros((1, te), F32)
    n_groups = (hi - lo + INVMAP_UNROLL - 1) // INVMAP_UNROLL
    cnt, tok, kk = lax.fori_loop(0, n_groups, body, (zero, zero, zero))
    valid = cnt > 0.5
    spare = n_tokens * TOP_K + lax.broadcasted_iota(jnp.int32, (1, te), 1)
    gid_ref[0] = jnp.where(valid, tok, 0.0).astype(jnp.int32)
    sid_ref[0] = jnp.where(valid, (kk * n_tokens + tok).astype(jnp.int32), spare)


def _invmap(tile_expert, rank0, lo, hi, code, n_tiles, te, n_tokens):
    n_tt, _, tm = code.shape
    spec = pl.BlockSpec((1, 1, te), lambda j, *_: (j, 0, 0))
    return pl.pallas_call(
        functools.partial(_invmap_kernel, tm=tm, te=te, n_tokens=n_tokens),
        grid_spec=pltpu.PrefetchScalarGridSpec(
            num_scalar_prefetch=4, grid=(n_tiles,),
            in_specs=[pl.BlockSpec((n_tt, N_EXPERTS, tm), lambda j, *_: (0, 0, 0))],
            out_specs=[spec, spec]),
        out_shape=[jax.ShapeDtypeStruct((n_tiles, 1, te), jnp.int32)] * 2,
        compiler_params=_params("parallel"),
        name="invmap",
    )(tile_expert, rank0, lo, hi, code)


def _moe_kernel(te_ref, gid0_ref, gidn_ref, sidp_ref, sidc_ref, x1r_ref, wgu_ref, bgu_ref, wd_ref, bd_ref, y4_ref,
                xg, yb, gsem, ssem, *, te, n_tiles):
    j = pl.program_id(0)
    slot = j % 2
    other = 1 - slot
    rows = te * ROW_SUBLANES

    def gather_row(ids_ref, par, r):
        src = pl.multiple_of(ids_ref[0, 0, r] * ROW_SUBLANES, ROW_SUBLANES)
        return pltpu.make_async_copy(x1r_ref.at[pl.ds(src, ROW_SUBLANES), :],
                                     xg.at[par, pl.ds(r * ROW_SUBLANES, ROW_SUBLANES), :], gsem.at[par])

    def scatter_row(ids_ref, par, r):
        dst = pl.multiple_of(ids_ref[0, 0, r] * ROW_SUBLANES, ROW_SUBLANES)
        return pltpu.make_async_copy(yb.at[par, pl.ds(r * ROW_SUBLANES, ROW_SUBLANES), :],
                                     y4_ref.at[pl.ds(dst, ROW_SUBLANES), :], ssem.at[par])

    def wait_gather(par):
        pltpu.make_async_copy(x1r_ref.at[pl.ds(0, rows), :], xg.at[par], gsem.at[par]).wait()

    def wait_scatter(par):
        pltpu.make_async_copy(yb.at[par], y4_ref.at[pl.ds(0, rows), :], ssem.at[par]).wait()

    @pl.when(j == 0)
    def _():
        yb[...] = jnp.zeros_like(yb)

        def first(r, c):
            gather_row(gid0_ref, 0, r).start()
            return c

        lax.fori_loop(0, te, first, 0)

    @pl.when(j > 0)
    def _():
        wait_scatter(slot)

    wait_gather(slot)

    x = jnp.concatenate([xg[slot, pl.ds(s, te, stride=ROW_SUBLANES), :].astype(BF16)
                         for s in range(ROW_SUBLANES)], axis=1)
    for r in range(te):
        gather_row(gidn_ref, other, r).start(priority=r % DMA_PRIORITIES)
    for r in range(te):
        scatter_row(sidp_ref, other, r).start(priority=r % DMA_PRIORITIES)
    hg = _dot(x, wgu_ref[:, :D_FF]) + bgu_ref[:, :D_FF]
    runtime_zero = jnp.minimum(te_ref[j], 0)
    probe = pltpu.bitcast(xg[slot + runtime_zero, 0:ROW_SUBLANES, :], jnp.int32) & runtime_zero
    plus_zero = pltpu.bitcast(probe, F32)[0:1, :]
    bias_up = bgu_ref[:, D_FF:] + jnp.concatenate([plus_zero] * (D_FF // LANES), axis=1)
    hu = _dot(x, wgu_ref[:, D_FF:]) + bias_up
    gate = jnp.minimum(hg, SWIGLU_LIMIT)
    up = jnp.clip(hu, -SWIGLU_LIMIT, SWIGLU_LIMIT)
    act = (up + 1.0) * gate * jax.nn.sigmoid(SWIGLU_ALPHA * gate)
    y = _dot(act.astype(BF16), wd_ref[...]) + bd_ref[...]
    for s in range(ROW_SUBLANES):
        yb[slot, pl.ds(s, te, stride=ROW_SUBLANES), :] = y[:, s * LANES:(s + 1) * LANES]

    @pl.when(j == n_tiles - 1)
    def _():
        wait_scatter(other)
        wait_gather(other)

        def last(r, c):
            scatter_row(sidc_ref, slot, r).start()
            return c

        lax.fori_loop(0, te, last, 0)
        wait_scatter(slot)


def _moe(tile_expert, gid, sid, x1r, w_gate_up, b_gate_up, w_down, b_down, n_tokens):
    n_tiles, _, te = gid.shape
    D = D_MODEL
    spare = (n_tokens * TOP_K + jnp.arange(te, dtype=jnp.int32)).reshape(1, 1, te)
    sid_prev = jnp.concatenate([spare, sid], axis=0)
    last = n_tiles - 1

    def ids(index):
        return pl.BlockSpec((1, 1, te), index, memory_space=pltpu.SMEM)

    return pl.pallas_call(
        functools.partial(_moe_kernel, te=te, n_tiles=n_tiles),
        grid_spec=pltpu.PrefetchScalarGridSpec(
            num_scalar_prefetch=1,
            grid=(n_tiles,),
            in_specs=[
                ids(lambda j, e: (0, 0, 0)),
                ids(lambda j, e: (jnp.minimum(j + 1, last), 0, 0)),
                ids(lambda j, e: (j, 0, 0)),
                ids(lambda j, e: (j + 1, 0, 0)),
                pl.BlockSpec(memory_space=pl.ANY),
                pl.BlockSpec((None, D, 2 * D_FF), lambda j, e: (e[j], 0, 0)),
                pl.BlockSpec((None, 1, 2 * D_FF), lambda j, e: (e[j], 0, 0)),
                pl.BlockSpec((None, D_FF, D), lambda j, e: (e[j], 0, 0)),
                pl.BlockSpec((None, 1, D), lambda j, e: (e[j], 0, 0)),
            ],
            out_specs=pl.BlockSpec(memory_space=pl.ANY),
            scratch_shapes=[pltpu.VMEM((2, te * ROW_SUBLANES, LANES), F32),
                            pltpu.VMEM((2, te * ROW_SUBLANES, LANES), F32),
                            pltpu.SemaphoreType.DMA((2,)), pltpu.SemaphoreType.DMA((2,))],
        ),
        out_shape=jax.ShapeDtypeStruct(((n_tokens * TOP_K + te) * ROW_SUBLANES, LANES), F32),
        compiler_params=_params("arbitrary"),
        name="moe",
    )(tile_expert, gid, gid, sid_prev, sid_prev, x1r,
      w_gate_up.astype(BF16), b_gate_up.reshape(N_EXPERTS, 1, 2 * D_FF).astype(F32),
      w_down.astype(BF16), b_down.reshape(N_EXPERTS, 1, D).astype(F32))


def _combine_kernel(x1_ref, x1b_ref, p_ref, wt_ref, wpg_ref, wpp_ref, g_ref, b_ref, y0_ref, y1_ref, y2_ref, y3_ref,
                    x2_ref, x2b_ref, *, tm):
    ple = jax.nn.sigmoid(_dot(x1b_ref[...], wpg_ref[...])) * _dot(p_ref[...].astype(BF16), wpp_ref[...])
    pieces = []
    for s in range(ROW_SUBLANES):
        acc = None
        for j, y_ref in enumerate((y0_ref, y1_ref, y2_ref, y3_ref)):
            term = wt_ref[:, j:j + 1] * y_ref[pl.ds(s, tm, stride=ROW_SUBLANES), :]
            acc = term if acc is None else acc + term
        pieces.append(acc)
    ffn = jnp.concatenate(pieces, axis=1)
    x2 = _layer_norm(DEEPNORM_ALPHA * x1_ref[...] + ffn + ple, g_ref[...], b_ref[...])
    x2_ref[...] = x2
    x2b_ref[...] = x2.astype(BF16)


def _combine(x1, x1b, p, wt, w_ple_gate, w_ple_proj, ln_g, ln_b, y4):
    T, D = x1.shape
    tm = min(ROW_TILE, T)
    n_tt = T // tm
    weights = [w_ple_gate.astype(BF16), w_ple_proj.astype(BF16),
               ln_g.reshape(1, D).astype(F32), ln_b.reshape(1, D).astype(F32)]

    def choice(j):
        return pl.BlockSpec((tm * ROW_SUBLANES, LANES), lambda i: (j * n_tt + i, 0))

    return pl.pallas_call(
        functools.partial(_combine_kernel, tm=tm),
        grid=(n_tt,),
        in_specs=[_rows(tm, D), _rows(tm, D), _rows(tm, PLE_DIM), _rows(tm, TOP_K)]
                 + [_full(w.shape) for w in weights] + [choice(j) for j in range(TOP_K)],
        out_specs=[_rows(tm, D), _rows(tm, D)],
        out_shape=[jax.ShapeDtypeStruct((T, D), F32), jax.ShapeDtypeStruct((T, D), BF16)],
        compiler_params=_params("parallel"),
        name="combine",
    )(x1, x1b, p, wt, *weights, y4, y4, y4, y4)


def _layer(x, xb, p, batch, seq, w):
    T, D = x.shape
    gu, vn, q, k, val, rs, la, sga, sgb = _inproj(xb, w["w_in"], w["w_gate_a"], w["b_gate_a"],
                                                  w["sgu_ln_g"], w["sgu_ln_b"])
    o = _gla(q, k, val, la, rs, w["gla_norm_g"], batch, seq)
    x1, x1b, x1r = _merge(x, gu, vn, o, sga, sgb, w["sgu_w"], w["sgu_b"], w["w_branch_a"], w["w_branch_b"],
                          w["w_out"], w["ln1_g"], w["ln1_b"])
    wt, counts, prefix, code = _router(x1, w["w_router"], w["b_router"])
    te = EXPERT_TILE
    n_tiles = (T * TOP_K) // te + N_EXPERTS
    tile_expert, rank0, lo, hi = _tile_plan(counts[0, :N_EXPERTS], prefix[:, 0, :N_EXPERTS], n_tiles, te)
    gid, sid = _invmap(tile_expert, rank0, lo, hi, code, n_tiles, te, T)
    y4 = _moe(tile_expert, gid, sid, x1r, w["w_gate_up"], w["b_gate_up"], w["w_down"], w["b_down"], T)
    return _combine(x1, x1b, p, wt, w["w_ple_gate"], w["w_ple_proj"], w["ln2_g"], w["ln2_b"], y4)


_WEIGHT_NAMES = ("w_in", "w_gate_a", "b_gate_a", "sgu_ln_g", "sgu_ln_b", "sgu_w", "sgu_b", "gla_norm_g",
                 "w_branch_a", "w_branch_b", "w_out", "ln1_g", "ln1_b", "w_router", "b_router", "w_gate_up",
                 "b_gate_up", "w_down", "b_down", "w_ple_proj", "w_ple_gate", "ln2_g", "ln2_b")


def kernel(x, p, w_in, w_gate_a, b_gate_a, sgu_ln_g, sgu_ln_b, sgu_w, sgu_b, gla_norm_g, w_branch_a, w_branch_b,
           w_out, ln1_g, ln1_b, w_router, b_router, w_gate_up, b_gate_up, w_down, b_down, w_ple_proj, w_ple_gate,
           ln2_g, ln2_b):
    stacked = dict(zip(_WEIGHT_NAMES, (
        w_in, w_gate_a, b_gate_a, sgu_ln_g, sgu_ln_b, sgu_w, sgu_b, gla_norm_g, w_branch_a, w_branch_b, w_out,
        ln1_g, ln1_b, w_router, b_router, w_gate_up, b_gate_up, w_down, b_down, w_ple_proj, w_ple_gate,
        ln2_g, ln2_b)))
    batch, seq, d = x.shape
    depth = w_in.shape[0]
    xf = x.reshape(batch * seq, d)
    xb = xf.astype(BF16)
    for i in range(depth):
        w = {name: arr[i] for name, arr in stacked.items()}
        xf, xb = _layer(xf, xb, p[i].reshape(batch * seq, PLE_DIM), batch, seq, w)
    return xf.reshape(batch, seq, d)
```

```python
import functools

import jax
import jax.numpy as jnp
from jax import lax
from jax.experimental import pallas as pl
from jax.experimental.pallas import tpu as pltpu

D_MODEL = 1024
DEPTH = 4
CHUNK = 64
SGU_WINDOW = 128
SGU_GROUPS = 8
GLA_HEADS = 4
GLA_DK = 128
GLA_DV = 256
GLA_GATE_RANK = 16
GLA_GATE_NORM = 16.0
N_EXPERTS = 32
TOP_K = 4
D_FF = 1024
SWIGLU_LIMIT = 7.0
SWIGLU_ALPHA = 1.702
PLE_DIM = 256
LN_EPS = 1e-5
DEEPNORM_ALPHA = (2 * DEPTH) ** 0.25
HK = GLA_HEADS * GLA_DK
HV = GLA_HEADS * GLA_DV
IN_SIZES = (D_MODEL, D_MODEL, HK, HK, HV, HV, GLA_GATE_RANK, D_MODEL, D_MODEL)

LANES = 128
ROW_SUBLANES = 8
EXPERT_LANES = 128
ROW_TILE = 256
INPROJ_TILE = 512
GLA_TILE = 512
EXPERT_TILE = 256
DMA_PRIORITIES = 2
INVMAP_UNROLL = 4
INVMAP_TILES = 8
VMEM_LIMIT = 56 * 1024 * 1024

F32 = jnp.float32
BF16 = jnp.bfloat16


def _dot(a, b, **kw):
    return jnp.dot(a, b, preferred_element_type=F32, **kw)


def _layer_norm(y, g, b):
    mu = jnp.mean(y, axis=-1, keepdims=True)
    d = y - mu
    var = jnp.mean(d * d, axis=-1, keepdims=True)
    return d * lax.rsqrt(var + LN_EPS) * g + b


def _params(*sem):
    return pltpu.CompilerParams(dimension_semantics=sem, vmem_limit_bytes=VMEM_LIMIT)


def _rows(tm, n):
    return pl.BlockSpec((tm, n), lambda i: (i, 0))


def _full(shape):
    nd = len(shape)
    return pl.BlockSpec(shape, lambda i: (0,) * nd)


def _resident(shape):
    nd = len(shape)
    return pl.BlockSpec(shape, lambda i: (0,) * nd, pipeline_mode=pl.Buffered(1))


def _inproj_kernel(x_ref, wu_ref, wv_ref, wq_ref, wk_ref, wval_ref, wr_ref, wglr_ref, wga_ref, wgb_ref,
                   wgate_ref, bgate_ref, lng_ref, lnb_ref,
                   gu_ref, vn_ref, q_ref, k_ref, val_ref, rs_ref, la_ref, sga_ref, sgb_ref):
    x = x_ref[...]
    gu_ref[...] = jax.nn.gelu(_dot(x, wu_ref[...])).astype(BF16)
    v = jax.nn.gelu(_dot(x, wv_ref[...]))
    vn_ref[...] = _layer_norm(v, lng_ref[...], lnb_ref[...]).astype(BF16)
    q_ref[...] = _dot(x, wq_ref[...]).astype(BF16)
    k_ref[...] = _dot(x, wk_ref[...]).astype(BF16)
    val_ref[...] = _dot(x, wval_ref[...]).astype(BF16)
    r = _dot(x, wr_ref[...])
    rs_ref[...] = (r * jax.nn.sigmoid(r)).astype(BF16)
    glr = _dot(x, wglr_ref[...])
    z = _dot(glr, wgate_ref[...], precision=lax.Precision.HIGHEST) + bgate_ref[...]
    la_ref[...] = (jnp.minimum(z, 0.0) - jnp.log(1.0 + jnp.exp(-jnp.abs(z)))) * (1.0 / GLA_GATE_NORM)
    sga_ref[...] = jax.nn.sigmoid(_dot(x, wga_ref[...])).astype(BF16)
    sgb_ref[...] = jax.nn.sigmoid(_dot(x, wgb_ref[...])).astype(BF16)


def _inproj(xb, w_in, w_gate_a, b_gate_a, ln_g, ln_b):
    T, D = xb.shape
    tm = min(INPROJ_TILE, T)
    splits = []
    o = 0
    for n in IN_SIZES:
        splits.append(w_in[:, o:o + n].astype(BF16))
        o += n
    wu, wv, wq, wk, wval, wr, wglr, wga, wgb = splits
    weights = [wu, wv, wq, wk, wval, wr, wglr, wga, wgb,
               w_gate_a.astype(F32), b_gate_a.reshape(1, HK).astype(F32),
               ln_g.reshape(1, D).astype(F32), ln_b.reshape(1, D).astype(F32)]
    out_shapes = [
        jax.ShapeDtypeStruct((T, D), BF16),
        jax.ShapeDtypeStruct((T, D), BF16),
        jax.ShapeDtypeStruct((T, HK), BF16),
        jax.ShapeDtypeStruct((T, HK), BF16),
        jax.ShapeDtypeStruct((T, HV), BF16),
        jax.ShapeDtypeStruct((T, HV), BF16),
        jax.ShapeDtypeStruct((T, HK), F32),
        jax.ShapeDtypeStruct((T, D), BF16),
        jax.ShapeDtypeStruct((T, D), BF16),
    ]
    return pl.pallas_call(
        _inproj_kernel,
        grid=(T // tm,),
        in_specs=[_rows(tm, D)] + [_resident(w.shape) for w in weights],
        out_specs=[_rows(tm, s.shape[1]) for s in out_shapes],
        out_shape=out_shapes,
        compiler_params=_params("parallel"),
        name="inproj",
    )(xb, *weights)


def _gla_kernel(q_ref, k_ref, v_ref, la_ref, rs_ref, g_ref, o_ref, st_ref, *, n_chunks):
    @pl.when(pl.program_id(1) == 0)
    def _():
        st_ref[...] = jnp.zeros_like(st_ref)

    ts = n_chunks * CHUNK
    row = lax.broadcasted_iota(jnp.int32, (CHUNK, CHUNK), 0)
    col = lax.broadcasted_iota(jnp.int32, (CHUNK, CHUNK), 1)
    causal = (row >= col)[None]
    nt = (((1,), (1,)), ((), ()))
    tn = (((0,), (0,)), ((), ()))
    bnt = (((2,), (2,)), ((0,), (0,)))
    bnn = (((2,), (1,)), ((0,), (0,)))

    pos = lax.broadcasted_iota(jnp.int32, (ts, 1), 0) % CHUNK
    b = la_ref[...]
    step = 1
    while step < CHUNK:
        b = b + jnp.where(pos >= step, pltpu.roll(b, step, axis=0), 0.0)
        step *= 2
    b = b.reshape(n_chunks, CHUNK, HK)
    b_end = b[:, CHUNK - 1:CHUNK, :]
    q = q_ref[...].astype(F32).reshape(n_chunks, CHUNK, HK) * (GLA_DK ** -0.5)
    k = k_ref[...].astype(F32).reshape(n_chunks, CHUNK, HK)
    qt_all = (q * jnp.exp(b)).astype(BF16)
    kt_all = (k * jnp.exp(-b)).astype(BF16)
    k_end_all = (k * jnp.exp(b_end - b)).astype(BF16)
    decay = jnp.exp(b_end)

    for h in range(GLA_HEADS):
        kc = slice(h * GLA_DK, (h + 1) * GLA_DK)
        vc = slice(h * GLA_DV, (h + 1) * GLA_DV)
        qt = qt_all[:, :, kc]
        vh = v_ref[:, vc].reshape(n_chunks, CHUNK, GLA_DV)
        att = lax.dot_general(qt, kt_all[:, :, kc], bnt, preferred_element_type=F32)
        att = jnp.where(causal, att, 0.0).astype(BF16)
        o_intra = lax.dot_general(att, vh, bnn, preferred_element_type=F32)
        st = st_ref[h]
        outs = []
        for c in range(n_chunks):
            outs.append(o_intra[c] + lax.dot_general(qt[c], st.astype(BF16), nt, preferred_element_type=F32))
            kv = lax.dot_general(vh[c], k_end_all[c, :, kc], tn, preferred_element_type=F32)
            st = decay[c, :, kc] * st + kv
        st_ref[h] = st
        o = jnp.concatenate(outs, axis=0)
        o = o * lax.rsqrt(jnp.mean(o * o, axis=-1, keepdims=True) + LN_EPS)
        o = o * g_ref[:, vc] * rs_ref[:, vc].astype(F32)
        o_ref[:, vc] = o.astype(BF16)


def _gla(q, k, v, la, rs, norm_g, batch, seq):
    T = q.shape[0]
    ts = min(GLA_TILE, seq)
    ns = seq // ts

    def rows(n):
        return pl.BlockSpec((ts, n), lambda b, s: (b * ns + s, 0))

    return pl.pallas_call(
        functools.partial(_gla_kernel, n_chunks=ts // CHUNK),
        grid=(batch, ns),
        in_specs=[rows(HK), rows(HK), rows(HV), rows(HK), rows(HV),
                  pl.BlockSpec((1, HV), lambda b, s: (0, 0))],
        out_specs=rows(HV),
        out_shape=jax.ShapeDtypeStruct((T, HV), BF16),
        scratch_shapes=[pltpu.VMEM((GLA_HEADS, GLA_DV, GLA_DK), F32)],
        compiler_params=_params("parallel", "arbitrary"),
        name="gla",
    )(q, k, v, la, rs, norm_g.reshape(1, HV).astype(F32))


def _merge_kernel(x_ref, gu_ref, vn_ref, o_ref, sga_ref, sgb_ref, ws_ref, bs_ref, wba_ref, wbb_ref, wout_ref,
                  g_ref, b_ref, x1_ref, x1b_ref, x1r_ref, a_ref, *, n_win, tm):
    row = lax.broadcasted_iota(jnp.int32, (SGU_WINDOW, SGU_WINDOW), 0) // CHUNK
    col = lax.broadcasted_iota(jnp.int32, (SGU_WINDOW, SGU_WINDOW), 1) // CHUNK
    keep = row >= col
    gd = D_MODEL // SGU_GROUPS
    for g in range(SGU_GROUPS):
        wm = jnp.where(keep, ws_ref[g], 0.0).astype(BF16)
        bias = bs_ref[:, g:g + 1]
        cs = slice(g * gd, (g + 1) * gd)
        for w in range(n_win):
            rs = slice(w * SGU_WINDOW, (w + 1) * SGU_WINDOW)
            s = _dot(wm, vn_ref[rs, cs]) + bias
            a_ref[rs, cs] = (gu_ref[rs, cs].astype(F32) * s).astype(BF16)
    h = (sga_ref[...].astype(F32) * _dot(a_ref[...], wba_ref[...])
         + sgb_ref[...].astype(F32) * _dot(o_ref[...], wbb_ref[...]))
    mix = _dot(h.astype(BF16), wout_ref[...])
    x1 = _layer_norm(DEEPNORM_ALPHA * x_ref[...] + mix, g_ref[...], b_ref[...])
    x1_ref[...] = x1
    x1b_ref[...] = x1.astype(BF16)
    for s in range(ROW_SUBLANES):
        x1r_ref[pl.ds(s, tm, stride=ROW_SUBLANES), :] = x1[:, s * LANES:(s + 1) * LANES]


def _merge(x, gu, vn, o, sga, sgb, sgu_w, sgu_b, w_ba, w_bb, w_out, ln_g, ln_b):
    T, D = x.shape
    tm = min(ROW_TILE, T)
    weights = [sgu_w.astype(F32), sgu_b.T.astype(F32), w_ba.astype(BF16), w_bb.astype(BF16), w_out.astype(BF16),
               ln_g.reshape(1, D).astype(F32), ln_b.reshape(1, D).astype(F32)]
    return pl.pallas_call(
        functools.partial(_merge_kernel, n_win=tm // SGU_WINDOW, tm=tm),
        grid=(T // tm,),
        in_specs=[_rows(tm, D)] * 6 + [_full(w.shape) for w in weights],
        out_specs=[_rows(tm, D), _rows(tm, D), _rows(tm * ROW_SUBLANES, LANES)],
        out_shape=[jax.ShapeDtypeStruct((T, D), F32), jax.ShapeDtypeStruct((T, D), BF16),
                   jax.ShapeDtypeStruct((T * ROW_SUBLANES, LANES), F32)],
        scratch_shapes=[pltpu.VMEM((tm, D), BF16)],
        compiler_params=_params("parallel"),
        name="merge",
    )(x, gu, vn, o, sga, sgb, *weights)


def _router_kernel(x_ref, xb_ref, wr_ref, br_ref, wt_ref, cnt_ref, pre_ref, code_ref, carry_ref, *, tm):
    @pl.when(pl.program_id(0) == 0)
    def _():
        carry_ref[...] = jnp.zeros_like(carry_ref)

    x_hi = xb_ref[...]
    x_lo = (x_ref[...] - x_hi.astype(F32)).astype(BF16)
    logits = (_dot(x_hi, wr_ref[0]) + _dot(x_lo, wr_ref[0]) + _dot(x_hi, wr_ref[1])) + br_ref[...]
    lane = lax.broadcasted_iota(jnp.int32, (tm, EXPERT_LANES), 1).astype(F32)
    work = logits
    hots, vals = [], []
    for _ in range(TOP_K):
        m = jnp.max(work, axis=-1, keepdims=True)
        first = jnp.min(jnp.where(work == m, lane, float(EXPERT_LANES)), axis=-1, keepdims=True)
        hot = lane == first
        hots.append(hot)
        vals.append(m)
        work = jnp.where(hot, -jnp.inf, work)
    exps = [jnp.exp(v - vals[0]) for v in vals]
    denom = exps[0] + exps[1] + exps[2] + exps[3]
    for j in range(TOP_K):
        wt_ref[:, j:j + 1] = exps[j] / denom

    chosen = hots[0] | hots[1] | hots[2] | hots[3]
    mask = chosen.astype(F32)
    which = hots[1].astype(F32) + 2.0 * hots[2].astype(F32) + 3.0 * hots[3].astype(F32)
    r = lax.broadcasted_iota(jnp.int32, (tm, tm), 0)
    c = lax.broadcasted_iota(jnp.int32, (tm, tm), 1)
    below = (r > c).astype(BF16)
    rank = _dot(below, mask.astype(BF16)) + carry_ref[...]
    pre_ref[0] = carry_ref[...]
    carry_ref[...] += jnp.sum(mask, axis=0, keepdims=True)
    cnt_ref[...] = carry_ref[...]
    code = jnp.where(chosen, TOP_K * rank + which, -float(TOP_K))
    code_ref[0] = code.T[:N_EXPERTS, :].astype(jnp.int32)


def _router(x1, x1b, w_router, b_router):
    T, D = x1.shape
    tm = min(ROW_TILE, T)
    n_tt = T // tm
    pad = EXPERT_LANES - N_EXPERTS
    wr = jnp.pad(w_router.astype(F32), ((0, 0), (0, pad)))
    wr_hi = wr.astype(BF16)
    wr = jnp.stack([wr_hi, (wr - wr_hi.astype(F32)).astype(BF16)])
    br = jnp.pad(b_router.reshape(1, N_EXPERTS).astype(F32), ((0, 0), (0, pad)), constant_values=-1e30)
    return pl.pallas_call(
        functools.partial(_router_kernel, tm=tm),
        grid=(n_tt,),
        in_specs=[_rows(tm, D), _rows(tm, D), _full((2, D, EXPERT_LANES)), _full((1, EXPERT_LANES))],
        out_specs=[_rows(tm, TOP_K), _full((1, EXPERT_LANES)),
                   pl.BlockSpec((1, 1, EXPERT_LANES), lambda i: (i, 0, 0)),
                   pl.BlockSpec((1, N_EXPERTS, tm), lambda i: (i, 0, 0))],
        out_shape=[jax.ShapeDtypeStruct((T, TOP_K), F32), jax.ShapeDtypeStruct((1, EXPERT_LANES), F32),
                   jax.ShapeDtypeStruct((n_tt, 1, EXPERT_LANES), F32),
                   jax.ShapeDtypeStruct((n_tt, N_EXPERTS, tm), jnp.int32)],
        scratch_shapes=[pltpu.VMEM((1, EXPERT_LANES), F32)],
        compiler_params=_params("arbitrary"),
        name="router",
    )(x1, x1b, wr, br)


def _tile_plan(counts, prefix, n_tiles, te):
    counts = counts.astype(jnp.int32)
    prefix = prefix.astype(jnp.int32)
    padded = ((counts + te - 1) // te) * te
    ends = jnp.cumsum(padded)
    offsets = ends - padded
    n_used = ends[-1] // te
    tile = jnp.arange(n_tiles, dtype=jnp.int32)
    used = tile < n_used
    start = jnp.minimum(tile, n_used - 1) * te
    expert = jnp.sum((ends[None, :] <= start[:, None]).astype(jnp.int32), axis=1)
    rank0 = start - offsets[expert]
    before = prefix[:, expert].T
    after = jnp.concatenate([prefix[1:], counts[None, :]], axis=0)[:, expert].T
    lo = jnp.sum((after <= rank0[:, None]).astype(jnp.int32), axis=1)
    hi = jnp.sum((before < (rank0 + te)[:, None]).astype(jnp.int32), axis=1)
    lo = jnp.where(used, lo, 0)
    hi = jnp.where(used, hi, 0)
    return expert, rank0, lo, hi


def _invmap_kernel(te_ref, r0_ref, lo_ref, hi_ref, code_ref, gid_ref, sid_ref, *, tm, te, n_tokens):
    for t in range(INVMAP_TILES):
        _invmap_tile(pl.program_id(0) * INVMAP_TILES + t, t, te_ref, r0_ref, lo_ref, hi_ref, code_ref,
                     gid_ref, sid_ref, tm=tm, te=te, n_tokens=n_tokens)


def _invmap_tile(j, out_row, te_ref, r0_ref, lo_ref, hi_ref, code_ref, gid_ref, sid_ref, *, tm, te, n_tokens):
    e = te_ref[j]
    want = r0_ref[j] + lax.broadcasted_iota(jnp.int32, (te, 1), 0)
    sub = lax.broadcasted_iota(jnp.int32, (ROW_SUBLANES, tm), 0)
    tloc = lax.broadcasted_iota(jnp.int32, (ROW_SUBLANES, tm), 1).astype(F32)
    nt = (((1,), (1,)), ((), ()))

    lo = lo_ref[j]
    hi = hi_ref[j]
    n_tt = code_ref.shape[0]

    def body(g, acc):
        cnt, tok, kk = acc
        for u in range(INVMAP_UNROLL):
            i = lo + g * INVMAP_UNROLL + u
            ic = jnp.minimum(i, n_tt - 1)
            code = code_ref[ic, pl.ds(e, 1), :]
            rank = jnp.where(i < hi, code >> 2, -1)
            hit = (rank == want).astype(F32).astype(BF16)
            which = (code & (TOP_K - 1)).astype(F32)
            sel = jnp.where(sub == 0, 1.0, jnp.where(sub == 1, tloc, jnp.where(sub == 2, which, 0.0))).astype(BF16)
            res = lax.dot_general(sel, hit, nt, preferred_element_type=F32)
            base = (ic * tm).astype(F32)
            cnt, tok, kk = cnt + res[0:1], tok + res[1:2] + base * res[0:1], kk + res[2:3]
        return cnt, tok, kk

    zero = jnp.zeros((1, te), F32)
    n_groups = (hi - lo + INVMAP_UNROLL - 1) // INVMAP_UNROLL
    cnt, tok, kk = lax.fori_loop(0, n_groups, body, (zero, zero, zero))
    valid = cnt > 0.5
    spare = n_tokens * TOP_K + lax.broadcasted_iota(jnp.int32, (1, te), 1)
    gid_ref[out_row] = jnp.where(valid, tok, 0.0).astype(jnp.int32)
    sid_ref[out_row] = jnp.where(valid, (kk * n_tokens + tok).astype(jnp.int32), spare)


def _invmap(tile_expert, rank0, lo, hi, code, n_tiles, te, n_tokens):
    n_tt, _, tm = code.shape
    spec = pl.BlockSpec((INVMAP_TILES, 1, te), lambda j, *_: (j, 0, 0))
    return pl.pallas_call(
        functools.partial(_invmap_kernel, tm=tm, te=te, n_tokens=n_tokens),
        grid_spec=pltpu.PrefetchScalarGridSpec(
            num_scalar_prefetch=4, grid=(n_tiles // INVMAP_TILES,),
            in_specs=[pl.BlockSpec((n_tt, N_EXPERTS, tm), lambda j, *_: (0, 0, 0))],
            out_specs=[spec, spec]),
        out_shape=[jax.ShapeDtypeStruct((n_tiles, 1, te), jnp.int32)] * 2,
        compiler_params=_params("parallel"),
        name="invmap",
    )(tile_expert, rank0, lo, hi, code)


def _moe_kernel(te_ref, gid0_ref, gidn_ref, sidp_ref, sidc_ref, x1r_ref, wgu_ref, bgu_ref, wd_ref, bd_ref, y4_ref,
                xg, yb, gsem, ssem, *, te, n_tiles):
    j = pl.program_id(0)
    slot = j % 2
    other = 1 - slot
    rows = te * ROW_SUBLANES

    def gather_row(ids_ref, par, r):
        src = pl.multiple_of(ids_ref[0, 0, r] * ROW_SUBLANES, ROW_SUBLANES)
        return pltpu.make_async_copy(x1r_ref.at[pl.ds(src, ROW_SUBLANES), :],
                                     xg.at[par, pl.ds(r * ROW_SUBLANES, ROW_SUBLANES), :], gsem.at[par])

    def scatter_row(ids_ref, par, r):
        dst = pl.multiple_of(ids_ref[0, 0, r] * ROW_SUBLANES, ROW_SUBLANES)
        return pltpu.make_async_copy(yb.at[par, pl.ds(r * ROW_SUBLANES, ROW_SUBLANES), :],
                                     y4_ref.at[pl.ds(dst, ROW_SUBLANES), :], ssem.at[par])

    def wait_gather(par):
        pltpu.make_async_copy(x1r_ref.at[pl.ds(0, rows), :], xg.at[par], gsem.at[par]).wait()

    def wait_scatter(par):
        pltpu.make_async_copy(yb.at[par], y4_ref.at[pl.ds(0, rows), :], ssem.at[par]).wait()

    @pl.when(j == 0)
    def _():
        yb[...] = jnp.zeros_like(yb)

        def first(r, c):
            gather_row(gid0_ref, 0, r).start()
            return c

        lax.fori_loop(0, te, first, 0)

    @pl.when(j > 0)
    def _():
        wait_scatter(slot)

    wait_gather(slot)

    x = jnp.concatenate([xg[slot, pl.ds(s, te, stride=ROW_SUBLANES), :].astype(BF16)
                         for s in range(ROW_SUBLANES)], axis=1)
    for r in range(te):
        gather_row(gidn_ref, other, r).start(priority=r % DMA_PRIORITIES)
    for r in range(te):
        scatter_row(sidp_ref, other, r).start(priority=r % DMA_PRIORITIES)
    hg = _dot(x, wgu_ref[:, :D_FF]) + bgu_ref[:, :D_FF]
    runtime_zero = jnp.minimum(te_ref[j], 0)
    probe = pltpu.bitcast(xg[slot + runtime_zero, 0:ROW_SUBLANES, :], jnp.int32) & runtime_zero
    plus_zero = pltpu.bitcast(probe, F32)[0:1, :]
    bias_up = bgu_ref[:, D_FF:] + jnp.concatenate([plus_zero] * (D_FF // LANES), axis=1)
    hu = _dot(x, wgu_ref[:, D_FF:]) + bias_up
    gate = jnp.minimum(hg, SWIGLU_LIMIT)
    up = jnp.clip(hu, -SWIGLU_LIMIT, SWIGLU_LIMIT)
    act = (up + 1.0) * gate * jax.nn.sigmoid(SWIGLU_ALPHA * gate)
    y = _dot(act.astype(BF16), wd_ref[...]) + bd_ref[...]
    for s in range(ROW_SUBLANES):
        yb[slot, pl.ds(s, te, stride=ROW_SUBLANES), :] = y[:, s * LANES:(s + 1) * LANES]

    @pl.when(j == n_tiles - 1)
    def _():
        wait_scatter(other)
        wait_gather(other)

        def last(r, c):
            scatter_row(sidc_ref, slot, r).start()
            return c

        lax.fori_loop(0, te, last, 0)
        wait_scatter(slot)


def _moe(tile_expert, gid, sid, x1r, w_gate_up, b_gate_up, w_down, b_down, n_tokens):
    n_tiles, _, te = gid.shape
    D = D_MODEL
    spare = (n_tokens * TOP_K + jnp.arange(te, dtype=jnp.int32)).reshape(1, 1, te)
    sid_prev = jnp.concatenate([spare, sid], axis=0)
    last = n_tiles - 1

    def ids(index):
        return pl.BlockSpec((1, 1, te), index, memory_space=pltpu.SMEM)

    return pl.pallas_call(
        functools.partial(_moe_kernel, te=te, n_tiles=n_tiles),
        grid_spec=pltpu.PrefetchScalarGridSpec(
            num_scalar_prefetch=1,
            grid=(n_tiles,),
            in_specs=[
                ids(lambda j, e: (0, 0, 0)),
                ids(lambda j, e: (jnp.minimum(j + 1, last), 0, 0)),
                ids(lambda j, e: (j, 0, 0)),
                ids(lambda j, e: (j + 1, 0, 0)),
                pl.BlockSpec(memory_space=pl.ANY),
                pl.BlockSpec((None, D, 2 * D_FF), lambda j, e: (e[j], 0, 0)),
                pl.BlockSpec((None, 1, 2 * D_FF), lambda j, e: (e[j], 0, 0)),
                pl.BlockSpec((None, D_FF, D), lambda j, e: (e[j], 0, 0)),
                pl.BlockSpec((None, 1, D), lambda j, e: (e[j], 0, 0)),
            ],
            out_specs=pl.BlockSpec(memory_space=pl.ANY),
            scratch_shapes=[pltpu.VMEM((2, te * ROW_SUBLANES, LANES), F32),
                            pltpu.VMEM((2, te * ROW_SUBLANES, LANES), F32),
                            pltpu.SemaphoreType.DMA((2,)), pltpu.SemaphoreType.DMA((2,))],
        ),
        out_shape=jax.ShapeDtypeStruct(((n_tokens * TOP_K + te) * ROW_SUBLANES, LANES), F32),
        compiler_params=_params("arbitrary"),
        name="moe",
    )(tile_expert, gid, gid, sid_prev, sid_prev, x1r,
      w_gate_up.astype(BF16), b_gate_up.reshape(N_EXPERTS, 1, 2 * D_FF).astype(F32),
      w_down.astype(BF16), b_down.reshape(N_EXPERTS, 1, D).astype(F32))


def _combine_kernel(x1_ref, x1b_ref, p_ref, wt_ref, wpg_ref, wpp_ref, g_ref, b_ref, y0_ref, y1_ref, y2_ref, y3_ref,
                    x2_ref, x2b_ref, *, tm):
    ple = jax.nn.sigmoid(_dot(x1b_ref[...], wpg_ref[...])) * _dot(p_ref[...].astype(BF16), wpp_ref[...])
    pieces = []
    for s in range(ROW_SUBLANES):
        acc = None
        for j, y_ref in enumerate((y0_ref, y1_ref, y2_ref, y3_ref)):
            term = wt_ref[:, j:j + 1] * y_ref[pl.ds(s, tm, stride=ROW_SUBLANES), :]
            acc = term if acc is None else acc + term
        pieces.append(acc)
    ffn = jnp.concatenate(pieces, axis=1)
    x2 = _layer_norm(DEEPNORM_ALPHA * x1_ref[...] + ffn + ple, g_ref[...], b_ref[...])
    x2_ref[...] = x2
    x2b_ref[...] = x2.astype(BF16)


def _combine(x1, x1b, p, wt, w_ple_gate, w_ple_proj, ln_g, ln_b, y4):
    T, D = x1.shape
    tm = min(ROW_TILE, T)
    n_tt = T // tm
    weights = [w_ple_gate.astype(BF16), w_ple_proj.astype(BF16),
               ln_g.reshape(1, D).astype(F32), ln_b.reshape(1, D).astype(F32)]

    def choice(j):
        return pl.BlockSpec((tm * ROW_SUBLANES, LANES), lambda i: (j * n_tt + i, 0))

    return pl.pallas_call(
        functools.partial(_combine_kernel, tm=tm),
        grid=(n_tt,),
        in_specs=[_rows(tm, D), _rows(tm, D), _rows(tm, PLE_DIM), _rows(tm, TOP_K)]
                 + [_full(w.shape) for w in weights] + [choice(j) for j in range(TOP_K)],
        out_specs=[_rows(tm, D), _rows(tm, D)],
        out_shape=[jax.ShapeDtypeStruct((T, D), F32), jax.ShapeDtypeStruct((T, D), BF16)],
        compiler_params=_params("parallel"),
        name="combine",
    )(x1, x1b, p, wt, *weights, y4, y4, y4, y4)


def _layer(x, xb, p, batch, seq, w):
    T, D = x.shape
    gu, vn, q, k, val, rs, la, sga, sgb = _inproj(xb, w["w_in"], w["w_gate_a"], w["b_gate_a"],
                                                  w["sgu_ln_g"], w["sgu_ln_b"])
    o = _gla(q, k, val, la, rs, w["gla_norm_g"], batch, seq)
    x1, x1b, x1r = _merge(x, gu, vn, o, sga, sgb, w["sgu_w"], w["sgu_b"], w["w_branch_a"], w["w_branch_b"],
                          w["w_out"], w["ln1_g"], w["ln1_b"])
    wt, counts, prefix, code = _router(x1, x1b, w["w_router"], w["b_router"])
    te = EXPERT_TILE
    n_tiles = (T * TOP_K) // te + N_EXPERTS
    tile_expert, rank0, lo, hi = _tile_plan(counts[0, :N_EXPERTS], prefix[:, 0, :N_EXPERTS], n_tiles, te)
    gid, sid = _invmap(tile_expert, rank0, lo, hi, code, n_tiles, te, T)
    y4 = _moe(tile_expert, gid, sid, x1r, w["w_gate_up"], w["b_gate_up"], w["w_down"], w["b_down"], T)
    return _combine(x1, x1b, p, wt, w["w_ple_gate"], w["w_ple_proj"], w["ln2_g"], w["ln2_b"], y4)


_WEIGHT_NAMES = ("w_in", "w_gate_a", "b_gate_a", "sgu_ln_g", "sgu_ln_b", "sgu_w", "sgu_b", "gla_norm_g",
                 "w_branch_a", "w_branch_b", "w_out", "ln1_g", "ln1_b", "w_router", "b_router", "w_gate_up",
                 "b_gate_up", "w_down", "b_down", "w_ple_proj", "w_ple_gate", "ln2_g", "ln2_b")


def kernel(x, p, w_in, w_gate_a, b_gate_a, sgu_ln_g, sgu_ln_b, sgu_w, sgu_b, gla_norm_g, w_branch_a, w_branch_b,
           w_out, ln1_g, ln1_b, w_router, b_router, w_gate_up, b_gate_up, w_down, b_down, w_ple_proj, w_ple_gate,
           ln2_g, ln2_b):
    stacked = dict(zip(_WEIGHT_NAMES, (
        w_in, w_gate_a, b_gate_a, sgu_ln_g, sgu_ln_b, sgu_w, sgu_b, gla_norm_g, w_branch_a, w_branch_b, w_out,
        ln1_g, ln1_b, w_router, b_router, w_gate_up, b_gate_up, w_down, b_down, w_ple_proj, w_ple_gate,
        ln2_g, ln2_b)))
    batch, seq, d = x.shape
    depth = w_in.shape[0]
    xf = x.reshape(batch * seq, d)
    xb = xf.astype(BF16)
    for i in range(depth):
        w = {name: arr[i] for name, arr in stacked.items()}
        xf, xb = _layer(xf, xb, p[i].reshape(batch * seq, PLE_DIM), batch, seq, w)
    return xf.reshape(batch, seq, d)
```

```python
import functools

import jax
import jax.numpy as jnp
from jax import lax
from jax.experimental import pallas as pl
from jax.experimental.pallas import tpu as pltpu

D_MODEL = 1024
DEPTH = 4
CHUNK = 64
SGU_WINDOW = 128
SGU_GROUPS = 8
GLA_HEADS = 4
GLA_DK = 128
GLA_DV = 256
GLA_GATE_RANK = 16
GLA_GATE_NORM = 16.0
N_EXPERTS = 32
TOP_K = 4
D_FF = 1024
SWIGLU_LIMIT = 7.0
SWIGLU_ALPHA = 1.702
PLE_DIM = 256
LN_EPS = 1e-5
DEEPNORM_ALPHA = (2 * DEPTH) ** 0.25
HK = GLA_HEADS * GLA_DK
HV = GLA_HEADS * GLA_DV
IN_SIZES = (D_MODEL, D_MODEL, HK, HK, HV, HV, GLA_GATE_RANK, D_MODEL, D_MODEL)

LANES = 128
ROW_SUBLANES = 8
EXPERT_LANES = 128
ROW_TILE = 256
INPROJ_TILE = 512
GLA_TILE = 512
EXPERT_TILE = 256
DMA_PRIORITIES = 2
INVMAP_UNROLL = 5
INVMAP_TILES = 8
VMEM_LIMIT = 56 * 1024 * 1024

F32 = jnp.float32
BF16 = jnp.bfloat16


def _dot(a, b, **kw):
    return jnp.dot(a, b, preferred_element_type=F32, **kw)


def _layer_norm(y, g, b):
    mu = jnp.mean(y, axis=-1, keepdims=True)
    d = y - mu
    var = jnp.mean(d * d, axis=-1, keepdims=True)
    return d * lax.rsqrt(var + LN_EPS) * g + b


def _params(*sem):
    return pltpu.CompilerParams(dimension_semantics=sem, vmem_limit_bytes=VMEM_LIMIT)


def _rows(tm, n):
    return pl.BlockSpec((tm, n), lambda i: (i, 0))


def _full(shape):
    nd = len(shape)
    return pl.BlockSpec(shape, lambda i: (0,) * nd)


def _resident(shape):
    nd = len(shape)
    return pl.BlockSpec(shape, lambda i: (0,) * nd, pipeline_mode=pl.Buffered(1))


def _inproj_kernel(x_ref, wu_ref, wv_ref, wq_ref, wk_ref, wval_ref, wr_ref, wglr_ref, wga_ref, wgb_ref,
                   wgate_ref, bgate_ref, lng_ref, lnb_ref,
                   gu_ref, vn_ref, q_ref, k_ref, val_ref, rs_ref, la_ref, sga_ref, sgb_ref):
    x = x_ref[...]
    gu_ref[...] = jax.nn.gelu(_dot(x, wu_ref[...])).astype(BF16)
    v = jax.nn.gelu(_dot(x, wv_ref[...]))
    vn_ref[...] = _layer_norm(v, lng_ref[...], lnb_ref[...]).astype(BF16)
    q_ref[...] = _dot(x, wq_ref[...]).astype(BF16)
    k_ref[...] = _dot(x, wk_ref[...]).astype(BF16)
    val_ref[...] = _dot(x, wval_ref[...]).astype(BF16)
    r = _dot(x, wr_ref[...])
    rs_ref[...] = (r * jax.nn.sigmoid(r)).astype(BF16)
    glr = _dot(x, wglr_ref[...])
    z = _dot(glr, wgate_ref[...], precision=lax.Precision.HIGHEST) + bgate_ref[...]
    la_ref[...] = (jnp.minimum(z, 0.0) - jnp.log(1.0 + jnp.exp(-jnp.abs(z)))) * (1.0 / GLA_GATE_NORM)
    sga_ref[...] = jax.nn.sigmoid(_dot(x, wga_ref[...])).astype(BF16)
    sgb_ref[...] = jax.nn.sigmoid(_dot(x, wgb_ref[...])).astype(BF16)


def _inproj(xb, w_in, w_gate_a, b_gate_a, ln_g, ln_b):
    T, D = xb.shape
    tm = min(INPROJ_TILE, T)
    splits = []
    o = 0
    for n in IN_SIZES:
        splits.append(w_in[:, o:o + n].astype(BF16))
        o += n
    wu, wv, wq, wk, wval, wr, wglr, wga, wgb = splits
    weights = [wu, wv, wq, wk, wval, wr, wglr, wga, wgb,
               w_gate_a.astype(F32), b_gate_a.reshape(1, HK).astype(F32),
               ln_g.reshape(1, D).astype(F32), ln_b.reshape(1, D).astype(F32)]
    out_shapes = [
        jax.ShapeDtypeStruct((T, D), BF16),
        jax.ShapeDtypeStruct((T, D), BF16),
        jax.ShapeDtypeStruct((T, HK), BF16),
        jax.ShapeDtypeStruct((T, HK), BF16),
        jax.ShapeDtypeStruct((T, HV), BF16),
        jax.ShapeDtypeStruct((T, HV), BF16),
        jax.ShapeDtypeStruct((T, HK), F32),
        jax.ShapeDtypeStruct((T, D), BF16),
        jax.ShapeDtypeStruct((T, D), BF16),
    ]
    return pl.pallas_call(
        _inproj_kernel,
        grid=(T // tm,),
        in_specs=[_rows(tm, D)] + [_resident(w.shape) for w in weights],
        out_specs=[_rows(tm, s.shape[1]) for s in out_shapes],
        out_shape=out_shapes,
        compiler_params=_params("parallel"),
        name="inproj",
    )(xb, *weights)


def _gla_kernel(q_ref, k_ref, v_ref, la_ref, rs_ref, g_ref, o_ref, st_ref, *, n_chunks):
    @pl.when(pl.program_id(1) == 0)
    def _():
        st_ref[...] = jnp.zeros_like(st_ref)

    ts = n_chunks * CHUNK
    row = lax.broadcasted_iota(jnp.int32, (CHUNK, CHUNK), 0)
    col = lax.broadcasted_iota(jnp.int32, (CHUNK, CHUNK), 1)
    causal = (row >= col)[None]
    nt = (((1,), (1,)), ((), ()))
    tn = (((0,), (0,)), ((), ()))
    bnt = (((2,), (2,)), ((0,), (0,)))
    bnn = (((2,), (1,)), ((0,), (0,)))

    pos = lax.broadcasted_iota(jnp.int32, (ts, 1), 0) % CHUNK
    b = la_ref[...]
    step = 1
    while step < CHUNK:
        b = b + jnp.where(pos >= step, pltpu.roll(b, step, axis=0), 0.0)
        step *= 2
    b = b.reshape(n_chunks, CHUNK, HK)
    b_end = b[:, CHUNK - 1:CHUNK, :]
    q = q_ref[...].astype(F32).reshape(n_chunks, CHUNK, HK) * (GLA_DK ** -0.5)
    k = k_ref[...].astype(F32).reshape(n_chunks, CHUNK, HK)
    qt_all = (q * jnp.exp(b)).astype(BF16)
    kt_all = (k * jnp.exp(-b)).astype(BF16)
    k_end_all = (k * jnp.exp(b_end - b)).astype(BF16)
    decay = jnp.exp(b_end)

    for h in range(GLA_HEADS):
        kc = slice(h * GLA_DK, (h + 1) * GLA_DK)
        vc = slice(h * GLA_DV, (h + 1) * GLA_DV)
        qt = qt_all[:, :, kc]
        vh = v_ref[:, vc].reshape(n_chunks, CHUNK, GLA_DV)
        att = lax.dot_general(qt, kt_all[:, :, kc], bnt, preferred_element_type=F32)
        att = jnp.where(causal, att, 0.0).astype(BF16)
        o_intra = lax.dot_general(att, vh, bnn, preferred_element_type=F32)
        st = st_ref[h]
        outs = []
        for c in range(n_chunks):
            outs.append(o_intra[c] + lax.dot_general(qt[c], st.astype(BF16), nt, preferred_element_type=F32))
            kv = lax.dot_general(vh[c], k_end_all[c, :, kc], tn, preferred_element_type=F32)
            st = decay[c, :, kc] * st + kv
        st_ref[h] = st
        o = jnp.concatenate(outs, axis=0)
        o = o * lax.rsqrt(jnp.mean(o * o, axis=-1, keepdims=True) + LN_EPS)
        o = o * g_ref[:, vc] * rs_ref[:, vc].astype(F32)
        o_ref[:, vc] = o.astype(BF16)


def _gla(q, k, v, la, rs, norm_g, batch, seq):
    T = q.shape[0]
    ts = min(GLA_TILE, seq)
    ns = seq // ts

    def rows(n):
        return pl.BlockSpec((ts, n), lambda b, s: (b * ns + s, 0))

    return pl.pallas_call(
        functools.partial(_gla_kernel, n_chunks=ts // CHUNK),
        grid=(batch, ns),
        in_specs=[rows(HK), rows(HK), rows(HV), rows(HK), rows(HV),
                  pl.BlockSpec((1, HV), lambda b, s: (0, 0))],
        out_specs=rows(HV),
        out_shape=jax.ShapeDtypeStruct((T, HV), BF16),
        scratch_shapes=[pltpu.VMEM((GLA_HEADS, GLA_DV, GLA_DK), F32)],
        compiler_params=_params("parallel", "arbitrary"),
        name="gla",
    )(q, k, v, la, rs, norm_g.reshape(1, HV).astype(F32))


def _merge_kernel(x_ref, gu_ref, vn_ref, o_ref, sga_ref, sgb_ref, ws_ref, bs_ref, wba_ref, wbb_ref, wout_ref,
                  g_ref, b_ref, x1_ref, x1b_ref, x1r_ref, a_ref, *, n_win, tm):
    row = lax.broadcasted_iota(jnp.int32, (SGU_WINDOW, SGU_WINDOW), 0) // CHUNK
    col = lax.broadcasted_iota(jnp.int32, (SGU_WINDOW, SGU_WINDOW), 1) // CHUNK
    keep = row >= col
    gd = D_MODEL // SGU_GROUPS
    for g in range(SGU_GROUPS):
        wm = jnp.where(keep, ws_ref[g], 0.0).astype(BF16)
        bias = bs_ref[:, g:g + 1]
        cs = slice(g * gd, (g + 1) * gd)
        for w in range(n_win):
            rs = slice(w * SGU_WINDOW, (w + 1) * SGU_WINDOW)
            s = _dot(wm, vn_ref[rs, cs]) + bias
            a_ref[rs, cs] = (gu_ref[rs, cs].astype(F32) * s).astype(BF16)
    h = (sga_ref[...].astype(F32) * _dot(a_ref[...], wba_ref[...])
         + sgb_ref[...].astype(F32) * _dot(o_ref[...], wbb_ref[...]))
    mix = _dot(h.astype(BF16), wout_ref[...])
    x1 = _layer_norm(DEEPNORM_ALPHA * x_ref[...] + mix, g_ref[...], b_ref[...])
    x1_ref[...] = x1
    x1b_ref[...] = x1.astype(BF16)
    for s in range(ROW_SUBLANES):
        x1r_ref[pl.ds(s, tm, stride=ROW_SUBLANES), :] = x1[:, s * LANES:(s + 1) * LANES]


def _merge(x, gu, vn, o, sga, sgb, sgu_w, sgu_b, w_ba, w_bb, w_out, ln_g, ln_b):
    T, D = x.shape
    tm = min(ROW_TILE, T)
    weights = [sgu_w.astype(F32), sgu_b.T.astype(F32), w_ba.astype(BF16), w_bb.astype(BF16), w_out.astype(BF16),
               ln_g.reshape(1, D).astype(F32), ln_b.reshape(1, D).astype(F32)]
    return pl.pallas_call(
        functools.partial(_merge_kernel, n_win=tm // SGU_WINDOW, tm=tm),
        grid=(T // tm,),
        in_specs=[_rows(tm, D)] * 6 + [_full(w.shape) for w in weights],
        out_specs=[_rows(tm, D), _rows(tm, D), _rows(tm * ROW_SUBLANES, LANES)],
        out_shape=[jax.ShapeDtypeStruct((T, D), F32), jax.ShapeDtypeStruct((T, D), BF16),
                   jax.ShapeDtypeStruct((T * ROW_SUBLANES, LANES), F32)],
        scratch_shapes=[pltpu.VMEM((tm, D), BF16)],
        compiler_params=_params("parallel"),
        name="merge",
    )(x, gu, vn, o, sga, sgb, *weights)


def _router_kernel(x_ref, xb_ref, wr_ref, br_ref, wt_ref, cnt_ref, pre_ref, code_ref, carry_ref, *, tm):
    @pl.when(pl.program_id(0) == 0)
    def _():
        carry_ref[...] = jnp.zeros_like(carry_ref)

    x_hi = xb_ref[...]
    x_lo = (x_ref[...] - x_hi.astype(F32)).astype(BF16)
    logits = (_dot(x_hi, wr_ref[0]) + _dot(x_lo, wr_ref[0]) + _dot(x_hi, wr_ref[1])) + br_ref[...]
    lane = lax.broadcasted_iota(jnp.int32, (tm, EXPERT_LANES), 1).astype(F32)
    work = logits
    hots, vals = [], []
    for _ in range(TOP_K):
        m = jnp.max(work, axis=-1, keepdims=True)
        first = jnp.min(jnp.where(work == m, lane, float(EXPERT_LANES)), axis=-1, keepdims=True)
        hot = lane == first
        hots.append(hot)
        vals.append(m)
        work = jnp.where(hot, -jnp.inf, work)
    exps = [jnp.exp(v - vals[0]) for v in vals]
    denom = exps[0] + exps[1] + exps[2] + exps[3]
    for j in range(TOP_K):
        wt_ref[:, j:j + 1] = exps[j] / denom

    chosen = hots[0] | hots[1] | hots[2] | hots[3]
    mask = chosen.astype(F32)
    which = hots[1].astype(F32) + 2.0 * hots[2].astype(F32) + 3.0 * hots[3].astype(F32)
    r = lax.broadcasted_iota(jnp.int32, (tm, tm), 0)
    c = lax.broadcasted_iota(jnp.int32, (tm, tm), 1)
    below = (r > c).astype(BF16)
    rank = _dot(below, mask.astype(BF16)) + carry_ref[...]
    pre_ref[0] = carry_ref[...]
    carry_ref[...] += jnp.sum(mask, axis=0, keepdims=True)
    cnt_ref[...] = carry_ref[...]
    code = jnp.where(chosen, TOP_K * rank + which, -float(TOP_K))
    code_ref[0] = code.T[:N_EXPERTS, :].astype(jnp.int32)


def _router(x1, x1b, w_router, b_router):
    T, D = x1.shape
    tm = min(ROW_TILE, T)
    n_tt = T // tm
    pad = EXPERT_LANES - N_EXPERTS
    wr = jnp.pad(w_router.astype(F32), ((0, 0), (0, pad)))
    wr_hi = wr.astype(BF16)
    wr = jnp.stack([wr_hi, (wr - wr_hi.astype(F32)).astype(BF16)])
    br = jnp.pad(b_router.reshape(1, N_EXPERTS).astype(F32), ((0, 0), (0, pad)), constant_values=-1e30)
    return pl.pallas_call(
        functools.partial(_router_kernel, tm=tm),
        grid=(n_tt,),
        in_specs=[_rows(tm, D), _rows(tm, D), _full((2, D, EXPERT_LANES)), _full((1, EXPERT_LANES))],
        out_specs=[_rows(tm, TOP_K), _full((1, EXPERT_LANES)),
                   pl.BlockSpec((1, 1, EXPERT_LANES), lambda i: (i, 0, 0)),
                   pl.BlockSpec((1, N_EXPERTS, tm), lambda i: (i, 0, 0))],
        out_shape=[jax.ShapeDtypeStruct((T, TOP_K), F32), jax.ShapeDtypeStruct((1, EXPERT_LANES), F32),
                   jax.ShapeDtypeStruct((n_tt, 1, EXPERT_LANES), F32),
                   jax.ShapeDtypeStruct((n_tt, N_EXPERTS, tm), jnp.int32)],
        scratch_shapes=[pltpu.VMEM((1, EXPERT_LANES), F32)],
        compiler_params=_params("arbitrary"),
        name="router",
    )(x1, x1b, wr, br)


def _tile_plan(counts, prefix, n_tiles, te):
    counts = counts.astype(jnp.int32)
    prefix = prefix.astype(jnp.int32)
    padded = ((counts + te - 1) // te) * te
    ends = jnp.cumsum(padded)
    offsets = ends - padded
    n_used = ends[-1] // te
    tile = jnp.arange(n_tiles, dtype=jnp.int32)
    used = tile < n_used
    start = jnp.minimum(tile, n_used - 1) * te
    expert = jnp.sum((ends[None, :] <= start[:, None]).astype(jnp.int32), axis=1)
    rank0 = start - offsets[expert]
    before = prefix[:, expert].T
    after = jnp.concatenate([prefix[1:], counts[None, :]], axis=0)[:, expert].T
    lo = jnp.sum((after <= rank0[:, None]).astype(jnp.int32), axis=1)
    hi = jnp.sum((before < (rank0 + te)[:, None]).astype(jnp.int32), axis=1)
    lo = jnp.where(used, lo, 0)
    hi = jnp.where(used, hi, 0)
    return expert, rank0, lo, hi, n_used


def _invmap_kernel(te_ref, r0_ref, lo_ref, hi_ref, code_ref, gid_ref, sid_ref, *, tm, te, n_tokens):
    for t in range(INVMAP_TILES):
        _invmap_tile(pl.program_id(0) * INVMAP_TILES + t, t, te_ref, r0_ref, lo_ref, hi_ref, code_ref,
                     gid_ref, sid_ref, tm=tm, te=te, n_tokens=n_tokens)


def _invmap_tile(j, out_row, te_ref, r0_ref, lo_ref, hi_ref, code_ref, gid_ref, sid_ref, *, tm, te, n_tokens):
    e = te_ref[j]
    want = r0_ref[j] + lax.broadcasted_iota(jnp.int32, (te, 1), 0)
    tloc = lax.broadcasted_iota(jnp.int32, (1, tm), 1)
    lo = lo_ref[j]
    hi = hi_ref[j]
    n_tt = code_ref.shape[0]

    def body(g, acc):
        for u in range(INVMAP_UNROLL):
            i = lo + g * INVMAP_UNROLL + u
            ic = jnp.minimum(i, n_tt - 1)
            code = code_ref[ic, pl.ds(e, 1), :]
            rank = jnp.where(i < hi, code >> 2, -1)
            tag = ((ic * tm + tloc) * TOP_K + (code & (TOP_K - 1)) + 1).astype(F32)
            sel = jnp.where(rank == want, tag, 0.0)
            for c in range(tm // LANES):
                acc = acc + sel[:, c * LANES:(c + 1) * LANES]
        return acc

    n_groups = (hi - lo + INVMAP_UNROLL - 1) // INVMAP_UNROLL
    acc = lax.fori_loop(0, n_groups, body, jnp.zeros((te, LANES), F32))
    found = jnp.sum(acc, axis=1, keepdims=True).astype(jnp.int32) - 1
    valid = found >= 0
    tok = found >> 2
    spare = n_tokens * TOP_K + (j % 2) * te + lax.broadcasted_iota(jnp.int32, (te, 1), 0)
    rows = pl.ds(out_row * te, te)
    gid_ref[rows, :] = jnp.where(valid, tok, 0)
    sid_ref[rows, :] = jnp.where(valid, (found & (TOP_K - 1)) * n_tokens + tok, spare)


def _invmap(tile_expert, rank0, lo, hi, code, n_tiles, te, n_tokens):
    n_tt, _, tm = code.shape
    spec = pl.BlockSpec((INVMAP_TILES * te, 1), lambda j, *_: (j, 0))
    gid, sid = pl.pallas_call(
        functools.partial(_invmap_kernel, tm=tm, te=te, n_tokens=n_tokens),
        grid_spec=pltpu.PrefetchScalarGridSpec(
            num_scalar_prefetch=4, grid=(n_tiles // INVMAP_TILES,),
            in_specs=[pl.BlockSpec((n_tt, N_EXPERTS, tm), lambda j, *_: (0, 0, 0))],
            out_specs=[spec, spec]),
        out_shape=[jax.ShapeDtypeStruct((n_tiles * te, 1), jnp.int32)] * 2,
        compiler_params=_params("parallel"),
        name="invmap",
    )(tile_expert, rank0, lo, hi, code)
    return gid.reshape(n_tiles, 1, te), sid.reshape(n_tiles, 1, te)


def _moe_kernel(te_ref, nu_ref, gid0_ref, gidn_ref, sidp_ref, sidc_ref, x1r_ref, wgu_ref, bgu_ref, wd_ref, bd_ref,
                y4_ref, xg, yb, gsem, ssem, *, te):
    j = pl.program_id(0)
    n_used = nu_ref[0]
    slot = j % 2
    other = 1 - slot
    out_slot = j % 3
    prev_slot = (j + 2) % 3
    rows = te * ROW_SUBLANES

    def gather_row(ids_ref, par, r):
        src = pl.multiple_of(ids_ref[0, 0, r] * ROW_SUBLANES, ROW_SUBLANES)
        return pltpu.make_async_copy(x1r_ref.at[pl.ds(src, ROW_SUBLANES), :],
                                     xg.at[par, pl.ds(r * ROW_SUBLANES, ROW_SUBLANES), :], gsem.at[par])

    def scatter_row(ids_ref, par, r):
        dst = pl.multiple_of(ids_ref[0, 0, r] * ROW_SUBLANES, ROW_SUBLANES)
        return pltpu.make_async_copy(yb.at[par, pl.ds(r * ROW_SUBLANES, ROW_SUBLANES), :],
                                     y4_ref.at[pl.ds(dst, ROW_SUBLANES), :], ssem.at[par])

    def wait_gather(par):
        pltpu.make_async_copy(x1r_ref.at[pl.ds(0, rows), :], xg.at[par], gsem.at[par]).wait()

    def wait_scatter(par):
        pltpu.make_async_copy(yb.at[par], y4_ref.at[pl.ds(0, rows), :], ssem.at[par]).wait()

    def step():
        @pl.when(j == 0)
        def _():
            yb[...] = jnp.zeros_like(yb)

            def first(r, c):
                gather_row(gid0_ref, 0, r).start()
                return c

            lax.fori_loop(0, te, first, 0)

        @pl.when(j >= 2)
        def _():
            wait_scatter(out_slot)

        wait_gather(slot)

        x = jnp.concatenate([xg[slot, pl.ds(s, te, stride=ROW_SUBLANES), :].astype(BF16)
                             for s in range(ROW_SUBLANES)], axis=1)
        for r in range(te):
            gather_row(gidn_ref, other, r).start(priority=r % DMA_PRIORITIES)
        for r in range(te):
            scatter_row(sidp_ref, prev_slot, r).start(priority=r % DMA_PRIORITIES)
        hg = _dot(x, wgu_ref[:, :D_FF]) + bgu_ref[:, :D_FF]
        runtime_zero = jnp.minimum(te_ref[j], 0)
        probe = pltpu.bitcast(xg[slot + runtime_zero, 0:ROW_SUBLANES, :], jnp.int32) & runtime_zero
        plus_zero = pltpu.bitcast(probe, F32)[0:1, :]
        bias_up = bgu_ref[:, D_FF:] + jnp.concatenate([plus_zero] * (D_FF // LANES), axis=1)
        hu = _dot(x, wgu_ref[:, D_FF:]) + bias_up
        gate = jnp.minimum(hg, SWIGLU_LIMIT)
        up = jnp.clip(hu, -SWIGLU_LIMIT, SWIGLU_LIMIT)
        act = (up + 1.0) * gate * jax.nn.sigmoid(SWIGLU_ALPHA * gate)
        y = _dot(act.astype(BF16), wd_ref[...]) + bd_ref[...]
        for s in range(ROW_SUBLANES):
            yb[out_slot, pl.ds(s, te, stride=ROW_SUBLANES), :] = y[:, s * LANES:(s + 1) * LANES]

        @pl.when(j == n_used - 1)
        def _():
            wait_scatter((j + 1) % 3)
            wait_scatter(prev_slot)
            wait_gather(other)

            def last(r, c):
                scatter_row(sidc_ref, out_slot, r).start()
                return c

            lax.fori_loop(0, te, last, 0)
            wait_scatter(out_slot)

    pl.when(j < n_used)(step)


def _moe(tile_expert, n_used, gid, sid, x1r, w_gate_up, b_gate_up, w_down, b_down, n_tokens):
    n_tiles, _, te = gid.shape
    D = D_MODEL
    spare = (n_tokens * TOP_K + te + jnp.arange(te, dtype=jnp.int32)).reshape(1, 1, te)
    sid_prev = jnp.concatenate([spare, sid], axis=0)
    last = n_tiles - 1

    def ids(index):
        return pl.BlockSpec((1, 1, te), index, memory_space=pltpu.SMEM)

    return pl.pallas_call(
        functools.partial(_moe_kernel, te=te),
        grid_spec=pltpu.PrefetchScalarGridSpec(
            num_scalar_prefetch=2,
            grid=(n_tiles,),
            in_specs=[
                ids(lambda j, e, nu: (0, 0, 0)),
                ids(lambda j, e, nu: (jnp.minimum(j + 1, last), 0, 0)),
                ids(lambda j, e, nu: (j, 0, 0)),
                ids(lambda j, e, nu: (j + 1, 0, 0)),
                pl.BlockSpec(memory_space=pl.ANY),
                pl.BlockSpec((None, D, 2 * D_FF), lambda j, e, nu: (e[j], 0, 0)),
                pl.BlockSpec((None, 1, 2 * D_FF), lambda j, e, nu: (e[j], 0, 0)),
                pl.BlockSpec((None, D_FF, D), lambda j, e, nu: (e[j], 0, 0)),
                pl.BlockSpec((None, 1, D), lambda j, e, nu: (e[j], 0, 0)),
            ],
            out_specs=pl.BlockSpec(memory_space=pl.ANY),
            scratch_shapes=[pltpu.VMEM((2, te * ROW_SUBLANES, LANES), F32),
                            pltpu.VMEM((3, te * ROW_SUBLANES, LANES), F32),
                            pltpu.SemaphoreType.DMA((2,)), pltpu.SemaphoreType.DMA((3,))],
        ),
        out_shape=jax.ShapeDtypeStruct(((n_tokens * TOP_K + 2 * te) * ROW_SUBLANES, LANES), F32),
        compiler_params=_params("arbitrary"),
        name="moe",
    )(tile_expert, n_used.reshape(1), gid, gid, sid_prev, sid_prev, x1r,
      w_gate_up.astype(BF16), b_gate_up.reshape(N_EXPERTS, 1, 2 * D_FF).astype(F32),
      w_down.astype(BF16), b_down.reshape(N_EXPERTS, 1, D).astype(F32))


def _combine_kernel(x1_ref, x1b_ref, p_ref, wt_ref, wpg_ref, wpp_ref, g_ref, b_ref, y0_ref, y1_ref, y2_ref, y3_ref,
                    x2_ref, x2b_ref, *, tm):
    ple = jax.nn.sigmoid(_dot(x1b_ref[...], wpg_ref[...])) * _dot(p_ref[...].astype(BF16), wpp_ref[...])
    pieces = []
    for s in range(ROW_SUBLANES):
        acc = None
        for j, y_ref in enumerate((y0_ref, y1_ref, y2_ref, y3_ref)):
            term = wt_ref[:, j:j + 1] * y_ref[pl.ds(s, tm, stride=ROW_SUBLANES), :]
            acc = term if acc is None else acc + term
        pieces.append(acc)
    ffn = jnp.concatenate(pieces, axis=1)
    x2 = _layer_norm(DEEPNORM_ALPHA * x1_ref[...] + ffn + ple, g_ref[...], b_ref[...])
    x2_ref[...] = x2
    x2b_ref[...] = x2.astype(BF16)


def _combine(x1, x1b, p, wt, w_ple_gate, w_ple_proj, ln_g, ln_b, y4):
    T, D = x1.shape
    tm = min(ROW_TILE, T)
    n_tt = T // tm
    weights = [w_ple_gate.astype(BF16), w_ple_proj.astype(BF16),
               ln_g.reshape(1, D).astype(F32), ln_b.reshape(1, D).astype(F32)]

    def choice(j):
        return pl.BlockSpec((tm * ROW_SUBLANES, LANES), lambda i: (j * n_tt + i, 0))

    return pl.pallas_call(
        functools.partial(_combine_kernel, tm=tm),
        grid=(n_tt,),
        in_specs=[_rows(tm, D), _rows(tm, D), _rows(tm, PLE_DIM), _rows(tm, TOP_K)]
                 + [_full(w.shape) for w in weights] + [choice(j) for j in range(TOP_K)],
        out_specs=[_rows(tm, D), _rows(tm, D)],
        out_shape=[jax.ShapeDtypeStruct((T, D), F32), jax.ShapeDtypeStruct((T, D), BF16)],
        compiler_params=_params("parallel"),
        name="combine",
    )(x1, x1b, p, wt, *weights, y4, y4, y4, y4)


def _layer(x, xb, p, batch, seq, w):
    T, D = x.shape
    gu, vn, q, k, val, rs, la, sga, sgb = _inproj(xb, w["w_in"], w["w_gate_a"], w["b_gate_a"],
                                                  w["sgu_ln_g"], w["sgu_ln_b"])
    o = _gla(q, k, val, la, rs, w["gla_norm_g"], batch, seq)
    x1, x1b, x1r = _merge(x, gu, vn, o, sga, sgb, w["sgu_w"], w["sgu_b"], w["w_branch_a"], w["w_branch_b"],
                          w["w_out"], w["ln1_g"], w["ln1_b"])
    wt, counts, prefix, code = _router(x1, x1b, w["w_router"], w["b_router"])
    te = EXPERT_TILE
    n_tiles = (T * TOP_K) // te + N_EXPERTS
    tile_expert, rank0, lo, hi, n_used = _tile_plan(counts[0, :N_EXPERTS], prefix[:, 0, :N_EXPERTS], n_tiles, te)
    gid, sid = _invmap(tile_expert, rank0, lo, hi, code, n_tiles, te, T)
    y4 = _moe(tile_expert, n_used, gid, sid, x1r, w["w_gate_up"], w["b_gate_up"], w["w_down"], w["b_down"], T)
    return _combine(x1, x1b, p, wt, w["w_ple_gate"], w["w_ple_proj"], w["ln2_g"], w["ln2_b"], y4)


_WEIGHT_NAMES = ("w_in", "w_gate_a", "b_gate_a", "sgu_ln_g", "sgu_ln_b", "sgu_w", "sgu_b", "gla_norm_g",
                 "w_branch_a", "w_branch_b", "w_out", "ln1_g", "ln1_b", "w_router", "b_router", "w_gate_up",
                 "b_gate_up", "w_down", "b_down", "w_ple_proj", "w_ple_gate", "ln2_g", "ln2_b")


def kernel(x, p, w_in, w_gate_a, b_gate_a, sgu_ln_g, sgu_ln_b, sgu_w, sgu_b, gla_norm_g, w_branch_a, w_branch_b,
           w_out, ln1_g, ln1_b, w_router, b_router, w_gate_up, b_gate_up, w_down, b_down, w_ple_proj, w_ple_gate,
           ln2_g, ln2_b):
    stacked = dict(zip(_WEIGHT_NAMES, (
        w_in, w_gate_a, b_gate_a, sgu_ln_g, sgu_ln_b, sgu_w, sgu_b, gla_norm_g, w_branch_a, w_branch_b, w_out,
        ln1_g, ln1_b, w_router, b_router, w_gate_up, b_gate_up, w_down, b_down, w_ple_proj, w_ple_gate,
        ln2_g, ln2_b)))
    batch, seq, d = x.shape
    depth = w_in.shape[0]
    xf = x.reshape(batch * seq, d)
    xb = xf.astype(BF16)
    for i in range(depth):
        w = {name: arr[i] for name, arr in stacked.items()}
        xf, xb = _layer(xf, xb, p[i].reshape(batch * seq, PLE_DIM), batch, seq, w)
    return xf.reshape(batch, seq, d)
```

```python
import functools

import jax
import jax.numpy as jnp
from jax import lax
from jax.experimental import pallas as pl
from jax.experimental.pallas import tpu as pltpu

D_MODEL = 1024
DEPTH = 4
CHUNK = 64
SGU_WINDOW = 128
SGU_GROUPS = 8
GLA_HEADS = 4
GLA_DK = 128
GLA_DV = 256
GLA_GATE_RANK = 16
GLA_GATE_NORM = 16.0
N_EXPERTS = 32
TOP_K = 4
D_FF = 1024
SWIGLU_LIMIT = 7.0
SWIGLU_ALPHA = 1.702
PLE_DIM = 256
LN_EPS = 1e-5
DEEPNORM_ALPHA = (2 * DEPTH) ** 0.25
HK = GLA_HEADS * GLA_DK
HV = GLA_HEADS * GLA_DV
IN_SIZES = (D_MODEL, D_MODEL, HK, HK, HV, HV, GLA_GATE_RANK, D_MODEL, D_MODEL)

LANES = 128
ROW_SUBLANES = 8
EXPERT_LANES = 128
ROW_TILE = 256
WIDE_TILE = 512
INPROJ_TILE = WIDE_TILE
GLA_TILE = 512
EXPERT_TILE = 256
DMA_PRIORITIES = 2
INVMAP_UNROLL = 5
INVMAP_TILES = 8
VMEM_LIMIT = 56 * 1024 * 1024

F32 = jnp.float32
BF16 = jnp.bfloat16


def _dot(a, b, **kw):
    return jnp.dot(a, b, preferred_element_type=F32, **kw)


def _layer_norm(y, g, b):
    mu = jnp.mean(y, axis=-1, keepdims=True)
    d = y - mu
    var = jnp.mean(d * d, axis=-1, keepdims=True)
    return d * lax.rsqrt(var + LN_EPS) * g + b


def _params(*sem):
    return pltpu.CompilerParams(dimension_semantics=sem, vmem_limit_bytes=VMEM_LIMIT)


def _rows(tm, n):
    return pl.BlockSpec((tm, n), lambda i: (i, 0))


def _full(shape):
    nd = len(shape)
    return pl.BlockSpec(shape, lambda i: (0,) * nd)


def _resident(shape):
    nd = len(shape)
    return pl.BlockSpec(shape, lambda i: (0,) * nd, pipeline_mode=pl.Buffered(1))


def _inproj_kernel(x_ref, wu_ref, wv_ref, wq_ref, wk_ref, wval_ref, wr_ref, wglr_ref, wga_ref, wgb_ref,
                   wgate_ref, bgate_ref, lng_ref, lnb_ref,
                   gu_ref, vn_ref, q_ref, k_ref, val_ref, rs_ref, la_ref, sga_ref, sgb_ref):
    x = x_ref[...]
    gu_ref[...] = jax.nn.gelu(_dot(x, wu_ref[...])).astype(BF16)
    v = jax.nn.gelu(_dot(x, wv_ref[...]))
    vn_ref[...] = _layer_norm(v, lng_ref[...], lnb_ref[...]).astype(BF16)
    q_ref[...] = _dot(x, wq_ref[...]).astype(BF16)
    k_ref[...] = _dot(x, wk_ref[...]).astype(BF16)
    val_ref[...] = _dot(x, wval_ref[...]).astype(BF16)
    r = _dot(x, wr_ref[...])
    rs_ref[...] = (r * jax.nn.sigmoid(r)).astype(BF16)
    glr = _dot(x, wglr_ref[...])
    z = _dot(glr, wgate_ref[...], precision=lax.Precision.HIGHEST) + bgate_ref[...]
    la_ref[...] = (jnp.minimum(z, 0.0) - jnp.log(1.0 + jnp.exp(-jnp.abs(z)))) * (1.0 / GLA_GATE_NORM)
    sga_ref[...] = jax.nn.sigmoid(_dot(x, wga_ref[...])).astype(BF16)
    sgb_ref[...] = jax.nn.sigmoid(_dot(x, wgb_ref[...])).astype(BF16)


def _inproj(xb, w_in, w_gate_a, b_gate_a, ln_g, ln_b):
    T, D = xb.shape
    tm = min(INPROJ_TILE, T)
    splits = []
    o = 0
    for n in IN_SIZES:
        splits.append(w_in[:, o:o + n].astype(BF16))
        o += n
    wu, wv, wq, wk, wval, wr, wglr, wga, wgb = splits
    weights = [wu, wv, wq, wk, wval, wr, wglr, wga, wgb,
               w_gate_a.astype(F32), b_gate_a.reshape(1, HK).astype(F32),
               ln_g.reshape(1, D).astype(F32), ln_b.reshape(1, D).astype(F32)]
    out_shapes = [
        jax.ShapeDtypeStruct((T, D), BF16),
        jax.ShapeDtypeStruct((T, D), BF16),
        jax.ShapeDtypeStruct((T, HK), BF16),
        jax.ShapeDtypeStruct((T, HK), BF16),
        jax.ShapeDtypeStruct((T, HV), BF16),
        jax.ShapeDtypeStruct((T, HV), BF16),
        jax.ShapeDtypeStruct((T, HK), F32),
        jax.ShapeDtypeStruct((T, D), BF16),
        jax.ShapeDtypeStruct((T, D), BF16),
    ]
    return pl.pallas_call(
        _inproj_kernel,
        grid=(T // tm,),
        in_specs=[_rows(tm, D)] + [_resident(w.shape) for w in weights],
        out_specs=[_rows(tm, s.shape[1]) for s in out_shapes],
        out_shape=out_shapes,
        compiler_params=_params("parallel"),
        name="inproj",
    )(xb, *weights)


def _gla_kernel(q_ref, k_ref, v_ref, la_ref, rs_ref, g_ref, o_ref, st_ref, *, n_chunks):
    @pl.when(pl.program_id(1) == 0)
    def _():
        st_ref[...] = jnp.zeros_like(st_ref)

    ts = n_chunks * CHUNK
    row = lax.broadcasted_iota(jnp.int32, (CHUNK, CHUNK), 0)
    col = lax.broadcasted_iota(jnp.int32, (CHUNK, CHUNK), 1)
    causal = (row >= col)[None]
    nt = (((1,), (1,)), ((), ()))
    tn = (((0,), (0,)), ((), ()))
    bnt = (((2,), (2,)), ((0,), (0,)))
    bnn = (((2,), (1,)), ((0,), (0,)))

    pos = lax.broadcasted_iota(jnp.int32, (ts, 1), 0) % CHUNK
    b = la_ref[...]
    step = 1
    while step < CHUNK:
        b = b + jnp.where(pos >= step, pltpu.roll(b, step, axis=0), 0.0)
        step *= 2
    b = b.reshape(n_chunks, CHUNK, HK)
    b_end = b[:, CHUNK - 1:CHUNK, :]
    q = q_ref[...].astype(F32).reshape(n_chunks, CHUNK, HK) * (GLA_DK ** -0.5)
    k = k_ref[...].astype(F32).reshape(n_chunks, CHUNK, HK)
    qt_all = (q * jnp.exp(b)).astype(BF16)
    kt_all = (k * jnp.exp(-b)).astype(BF16)
    k_end_all = (k * jnp.exp(b_end - b)).astype(BF16)
    decay = jnp.exp(b_end)

    for h in range(GLA_HEADS):
        kc = slice(h * GLA_DK, (h + 1) * GLA_DK)
        vc = slice(h * GLA_DV, (h + 1) * GLA_DV)
        qt = qt_all[:, :, kc]
        vh = v_ref[:, vc].reshape(n_chunks, CHUNK, GLA_DV)
        att = lax.dot_general(qt, kt_all[:, :, kc], bnt, preferred_element_type=F32)
        att = jnp.where(causal, att, 0.0).astype(BF16)
        o_intra = lax.dot_general(att, vh, bnn, preferred_element_type=F32)
        st = st_ref[h]
        outs = []
        for c in range(n_chunks):
            outs.append(o_intra[c] + lax.dot_general(qt[c], st.astype(BF16), nt, preferred_element_type=F32))
            kv = lax.dot_general(vh[c], k_end_all[c, :, kc], tn, preferred_element_type=F32)
            st = decay[c, :, kc] * st + kv
        st_ref[h] = st
        o = jnp.concatenate(outs, axis=0)
        o = o * lax.rsqrt(jnp.mean(o * o, axis=-1, keepdims=True) + LN_EPS)
        o = o * g_ref[:, vc] * rs_ref[:, vc].astype(F32)
        o_ref[:, vc] = o.astype(BF16)


def _gla(q, k, v, la, rs, norm_g, batch, seq):
    T = q.shape[0]
    ts = min(GLA_TILE, seq)
    ns = seq // ts

    def rows(n):
        return pl.BlockSpec((ts, n), lambda b, s: (b * ns + s, 0))

    return pl.pallas_call(
        functools.partial(_gla_kernel, n_chunks=ts // CHUNK),
        grid=(batch, ns),
        in_specs=[rows(HK), rows(HK), rows(HV), rows(HK), rows(HV),
                  pl.BlockSpec((1, HV), lambda b, s: (0, 0))],
        out_specs=rows(HV),
        out_shape=jax.ShapeDtypeStruct((T, HV), BF16),
        scratch_shapes=[pltpu.VMEM((GLA_HEADS, GLA_DV, GLA_DK), F32)],
        compiler_params=_params("parallel", "arbitrary"),
        name="gla",
    )(q, k, v, la, rs, norm_g.reshape(1, HV).astype(F32))


def _merge_kernel(x_ref, gu_ref, vn_ref, o_ref, sga_ref, sgb_ref, ws_ref, bs_ref, wba_ref, wbb_ref, wout_ref,
                  g_ref, b_ref, x1_ref, x1b_ref, x1r_ref, a_ref, *, n_win, tm):
    row = lax.broadcasted_iota(jnp.int32, (SGU_WINDOW, SGU_WINDOW), 0) // CHUNK
    col = lax.broadcasted_iota(jnp.int32, (SGU_WINDOW, SGU_WINDOW), 1) // CHUNK
    keep = row >= col
    gd = D_MODEL // SGU_GROUPS
    for g in range(SGU_GROUPS):
        wm = jnp.where(keep, ws_ref[g], 0.0).astype(BF16)
        bias = bs_ref[:, g:g + 1]
        cs = slice(g * gd, (g + 1) * gd)
        wins = [slice(w * SGU_WINDOW, (w + 1) * SGU_WINDOW) for w in range(n_win)]
        s_all = _dot(wm, jnp.concatenate([vn_ref[rs, cs] for rs in wins], axis=1))
        for w, rs in enumerate(wins):
            s = s_all[:, w * gd:(w + 1) * gd] + bias
            a_ref[rs, cs] = (gu_ref[rs, cs].astype(F32) * s).astype(BF16)
    h = (sga_ref[...].astype(F32) * _dot(a_ref[...], wba_ref[...])
         + sgb_ref[...].astype(F32) * _dot(o_ref[...], wbb_ref[...]))
    mix = _dot(h.astype(BF16), wout_ref[...])
    x1 = _layer_norm(DEEPNORM_ALPHA * x_ref[...] + mix, g_ref[...], b_ref[...])
    x1_ref[...] = x1
    x1b_ref[...] = x1.astype(BF16)
    for s in range(ROW_SUBLANES):
        x1r_ref[pl.ds(s, tm, stride=ROW_SUBLANES), :] = x1[:, s * LANES:(s + 1) * LANES]


def _merge(x, gu, vn, o, sga, sgb, sgu_w, sgu_b, w_ba, w_bb, w_out, ln_g, ln_b):
    T, D = x.shape
    tm = min(WIDE_TILE, T)
    weights = [sgu_w.astype(F32), sgu_b.T.astype(F32), w_ba.astype(BF16), w_bb.astype(BF16), w_out.astype(BF16),
               ln_g.reshape(1, D).astype(F32), ln_b.reshape(1, D).astype(F32)]
    return pl.pallas_call(
        functools.partial(_merge_kernel, n_win=tm // SGU_WINDOW, tm=tm),
        grid=(T // tm,),
        in_specs=[_rows(tm, D)] * 6 + [_resident(w.shape) for w in weights],
        out_specs=[_rows(tm, D), _rows(tm, D), _rows(tm * ROW_SUBLANES, LANES)],
        out_shape=[jax.ShapeDtypeStruct((T, D), F32), jax.ShapeDtypeStruct((T, D), BF16),
                   jax.ShapeDtypeStruct((T * ROW_SUBLANES, LANES), F32)],
        scratch_shapes=[pltpu.VMEM((tm, D), BF16)],
        compiler_params=_params("parallel"),
        name="merge",
    )(x, gu, vn, o, sga, sgb, *weights)


def _router_kernel(x_ref, xb_ref, wr_ref, br_ref, wt_ref, cnt_ref, pre_ref, code_ref, carry_ref, *, tm):
    @pl.when(pl.program_id(0) == 0)
    def _():
        carry_ref[...] = jnp.zeros_like(carry_ref)

    x_hi = xb_ref[...]
    x_lo = (x_ref[...] - x_hi.astype(F32)).astype(BF16)
    logits = (_dot(x_hi, wr_ref[0]) + _dot(x_lo, wr_ref[0]) + _dot(x_hi, wr_ref[1])) + br_ref[...]
    lane = lax.broadcasted_iota(jnp.int32, (tm, EXPERT_LANES), 1).astype(F32)
    work = logits
    hots, vals = [], []
    for _ in range(TOP_K):
        m = jnp.max(work, axis=-1, keepdims=True)
        first = jnp.min(jnp.where(work == m, lane, float(EXPERT_LANES)), axis=-1, keepdims=True)
        hot = lane == first
        hots.append(hot)
        vals.append(m)
        work = jnp.where(hot, -jnp.inf, work)
    exps = [jnp.exp(v - vals[0]) for v in vals]
    denom = exps[0] + exps[1] + exps[2] + exps[3]
    for j in range(TOP_K):
        wt_ref[:, j:j + 1] = exps[j] / denom

    chosen = hots[0] | hots[1] | hots[2] | hots[3]
    mask = chosen.astype(F32)
    which = hots[1].astype(F32) + 2.0 * hots[2].astype(F32) + 3.0 * hots[3].astype(F32)
    r = lax.broadcasted_iota(jnp.int32, (tm, tm), 0)
    c = lax.broadcasted_iota(jnp.int32, (tm, tm), 1)
    below = (r > c).astype(BF16)
    rank = _dot(below, mask.astype(BF16)) + carry_ref[...]
    pre_ref[0] = carry_ref[...]
    carry_ref[...] += jnp.sum(mask, axis=0, keepdims=True)
    cnt_ref[...] = carry_ref[...]
    code = jnp.where(chosen, TOP_K * rank + which, -float(TOP_K))
    code_ref[0] = code.T[:N_EXPERTS, :].astype(jnp.int32)


def _router(x1, x1b, w_router, b_router):
    T, D = x1.shape
    tm = min(ROW_TILE, T)
    n_tt = T // tm
    pad = EXPERT_LANES - N_EXPERTS
    wr = jnp.pad(w_router.astype(F32), ((0, 0), (0, pad)))
    wr_hi = wr.astype(BF16)
    wr = jnp.stack([wr_hi, (wr - wr_hi.astype(F32)).astype(BF16)])
    br = jnp.pad(b_router.reshape(1, N_EXPERTS).astype(F32), ((0, 0), (0, pad)), constant_values=-1e30)
    return pl.pallas_call(
        functools.partial(_router_kernel, tm=tm),
        grid=(n_tt,),
        in_specs=[_rows(tm, D), _rows(tm, D), _full((2, D, EXPERT_LANES)), _full((1, EXPERT_LANES))],
        out_specs=[_rows(tm, TOP_K), _full((1, EXPERT_LANES)),
                   pl.BlockSpec((1, 1, EXPERT_LANES), lambda i: (i, 0, 0)),
                   pl.BlockSpec((1, N_EXPERTS, tm), lambda i: (i, 0, 0))],
        out_shape=[jax.ShapeDtypeStruct((T, TOP_K), F32), jax.ShapeDtypeStruct((1, EXPERT_LANES), F32),
                   jax.ShapeDtypeStruct((n_tt, 1, EXPERT_LANES), F32),
                   jax.ShapeDtypeStruct((n_tt, N_EXPERTS, tm), jnp.int32)],
        scratch_shapes=[pltpu.VMEM((1, EXPERT_LANES), F32)],
        compiler_params=_params("arbitrary"),
        name="router",
    )(x1, x1b, wr, br)


def _tile_plan(counts, prefix, n_tiles, te):
    counts = counts.astype(jnp.int32)
    prefix = prefix.astype(jnp.int32)
    padded = ((counts + te - 1) // te) * te
    ends = jnp.cumsum(padded)
    offsets = ends - padded
    n_used = ends[-1] // te
    tile = jnp.arange(n_tiles, dtype=jnp.int32)
    used = tile < n_used
    start = jnp.minimum(tile, n_used - 1) * te
    expert = jnp.sum((ends[None, :] <= start[:, None]).astype(jnp.int32), axis=1)
    rank0 = start - offsets[expert]
    before = prefix[:, expert].T
    after = jnp.concatenate([prefix[1:], counts[None, :]], axis=0)[:, expert].T
    lo = jnp.sum((after <= rank0[:, None]).astype(jnp.int32), axis=1)
    hi = jnp.sum((before < (rank0 + te)[:, None]).astype(jnp.int32), axis=1)
    lo = jnp.where(used, lo, 0)
    hi = jnp.where(used, hi, 0)
    return expert, rank0, lo, hi, n_used


def _invmap_kernel(te_ref, r0_ref, lo_ref, hi_ref, code_ref, gid_ref, sid_ref, *, tm, te, n_tokens):
    for t in range(INVMAP_TILES):
        _invmap_tile(pl.program_id(0) * INVMAP_TILES + t, t, te_ref, r0_ref, lo_ref, hi_ref, code_ref,
                     gid_ref, sid_ref, tm=tm, te=te, n_tokens=n_tokens)


def _invmap_tile(j, out_row, te_ref, r0_ref, lo_ref, hi_ref, code_ref, gid_ref, sid_ref, *, tm, te, n_tokens):
    e = te_ref[j]
    want = r0_ref[j] + lax.broadcasted_iota(jnp.int32, (te, 1), 0)
    tloc = lax.broadcasted_iota(jnp.int32, (1, tm), 1)
    lo = lo_ref[j]
    hi = hi_ref[j]
    n_tt = code_ref.shape[0]

    def body(g, acc):
        for u in range(INVMAP_UNROLL):
            i = lo + g * INVMAP_UNROLL + u
            ic = jnp.minimum(i, n_tt - 1)
            code = code_ref[ic, pl.ds(e, 1), :]
            rank = jnp.where(i < hi, code >> 2, -1)
            tag = ((ic * tm + tloc) * TOP_K + (code & (TOP_K - 1)) + 1).astype(F32)
            sel = jnp.where(rank == want, tag, 0.0)
            for c in range(tm // LANES):
                acc = acc + sel[:, c * LANES:(c + 1) * LANES]
        return acc

    n_groups = (hi - lo + INVMAP_UNROLL - 1) // INVMAP_UNROLL
    acc = lax.fori_loop(0, n_groups, body, jnp.zeros((te, LANES), F32))
    found = jnp.sum(acc, axis=1, keepdims=True).astype(jnp.int32) - 1
    valid = found >= 0
    tok = found >> 2
    spare = n_tokens * TOP_K + (j % 2) * te + lax.broadcasted_iota(jnp.int32, (te, 1), 0)
    rows = pl.ds(out_row * te, te)
    gid_ref[rows, :] = jnp.where(valid, tok, 0)
    sid_ref[rows, :] = jnp.where(valid, (found & (TOP_K - 1)) * n_tokens + tok, spare)


def _invmap(tile_expert, rank0, lo, hi, code, n_tiles, te, n_tokens):
    n_tt, _, tm = code.shape
    spec = pl.BlockSpec((INVMAP_TILES * te, 1), lambda j, *_: (j, 0))
    gid, sid = pl.pallas_call(
        functools.partial(_invmap_kernel, tm=tm, te=te, n_tokens=n_tokens),
        grid_spec=pltpu.PrefetchScalarGridSpec(
            num_scalar_prefetch=4, grid=(n_tiles // INVMAP_TILES,),
            in_specs=[pl.BlockSpec((n_tt, N_EXPERTS, tm), lambda j, *_: (0, 0, 0))],
            out_specs=[spec, spec]),
        out_shape=[jax.ShapeDtypeStruct((n_tiles * te, 1), jnp.int32)] * 2,
        compiler_params=_params("parallel"),
        name="invmap",
    )(tile_expert, rank0, lo, hi, code)
    return gid.reshape(n_tiles, 1, te), sid.reshape(n_tiles, 1, te)


def _moe_kernel(te_ref, nu_ref, gid0_ref, gidn_ref, sidp_ref, sidc_ref, x1r_ref, wgu_ref, bgu_ref, wd_ref, bd_ref,
                y4_ref, xg, yb, gsem, ssem, *, te):
    j = pl.program_id(0)
    n_used = nu_ref[0]
    slot = j % 2
    other = 1 - slot
    out_slot = j % 3
    prev_slot = (j + 2) % 3
    rows = te * ROW_SUBLANES
    n_rows = y4_ref.shape[0] - 2 * rows

    def gather_row(ids_ref, par, r):
        src = pl.multiple_of(ids_ref[0, 0, r] * ROW_SUBLANES, ROW_SUBLANES)
        return pltpu.make_async_copy(x1r_ref.at[pl.ds(src, ROW_SUBLANES), :],
                                     xg.at[par, pl.ds(r * ROW_SUBLANES, ROW_SUBLANES), :], gsem.at[par])

    def scatter_row(ids_ref, par, r):
        dst = pl.multiple_of(ids_ref[0, 0, r] * ROW_SUBLANES, ROW_SUBLANES)
        return pltpu.make_async_copy(yb.at[par, pl.ds(r * ROW_SUBLANES, ROW_SUBLANES), :],
                                     y4_ref.at[pl.ds(dst, ROW_SUBLANES), :], ssem.at[par])

    def wait_gather(par):
        pltpu.make_async_copy(x1r_ref.at[pl.ds(0, rows), :], xg.at[par], gsem.at[par]).wait()

    def wait_scatter(par):
        pltpu.make_async_copy(yb.at[par], y4_ref.at[pl.ds(0, rows), :], ssem.at[par]).wait()

    def step():
        @pl.when(j == 0)
        def _():
            yb[...] = jnp.zeros_like(yb)
            for region in range(2):
                spare = pltpu.make_async_copy(yb.at[region], y4_ref.at[pl.ds(n_rows + region * rows, rows), :],
                                              ssem.at[region])
                spare.start()
                spare.wait()

            def first(r, c):
                gather_row(gid0_ref, 0, r).start()
                return c

            lax.fori_loop(0, te, first, 0)

        @pl.when(j >= 2)
        def _():
            wait_scatter(out_slot)

        wait_gather(slot)

        x = jnp.concatenate([xg[slot, pl.ds(s, te, stride=ROW_SUBLANES), :].astype(BF16)
                             for s in range(ROW_SUBLANES)], axis=1)
        for r in range(te):
            gather_row(gidn_ref, other, r).start(priority=r % DMA_PRIORITIES)
        for r in range(te):
            scatter_row(sidp_ref, prev_slot, r).start(priority=r % DMA_PRIORITIES)
        hg = _dot(x, wgu_ref[:, :D_FF]) + bgu_ref[:, :D_FF]
        runtime_zero = jnp.minimum(te_ref[j], 0)
        probe = pltpu.bitcast(xg[slot + runtime_zero, 0:ROW_SUBLANES, :], jnp.int32) & runtime_zero
        plus_zero = pltpu.bitcast(probe, F32)[0:1, :]
        bias_up = bgu_ref[:, D_FF:] + jnp.concatenate([plus_zero] * (D_FF // LANES), axis=1)
        hu = _dot(x, wgu_ref[:, D_FF:]) + bias_up
        gate = jnp.minimum(hg, SWIGLU_LIMIT)
        up = jnp.clip(hu, -SWIGLU_LIMIT, SWIGLU_LIMIT)
        act = (up + 1.0) * gate * jax.nn.sigmoid(SWIGLU_ALPHA * gate)
        y = _dot(act.astype(BF16), wd_ref[...]) + bd_ref[...]
        for s in range(ROW_SUBLANES):
            yb[out_slot, pl.ds(s, te, stride=ROW_SUBLANES), :] = y[:, s * LANES:(s + 1) * LANES]

        @pl.when(j == n_used - 1)
        def _():
            wait_scatter((j + 1) % 3)
            wait_scatter(prev_slot)
            wait_gather(other)

            def last(r, c):
                scatter_row(sidc_ref, out_slot, r).start()
                return c

            lax.fori_loop(0, te, last, 0)
            wait_scatter(out_slot)

    pl.when(j < n_used)(step)


def _moe(tile_expert, n_used, gid, sid, x1r, w_gate_up, b_gate_up, w_down, b_down, n_tokens):
    n_tiles, _, te = gid.shape
    D = D_MODEL
    spare = (n_tokens * TOP_K + te + jnp.arange(te, dtype=jnp.int32)).reshape(1, 1, te)
    sid_prev = jnp.concatenate([spare, sid], axis=0)
    last = n_tiles - 1

    def ids(index):
        return pl.BlockSpec((1, 1, te), index, memory_space=pltpu.SMEM)

    return pl.pallas_call(
        functools.partial(_moe_kernel, te=te),
        grid_spec=pltpu.PrefetchScalarGridSpec(
            num_scalar_prefetch=2,
            grid=(n_tiles,),
            in_specs=[
                ids(lambda j, e, nu: (0, 0, 0)),
                ids(lambda j, e, nu: (jnp.minimum(j + 1, last), 0, 0)),
                ids(lambda j, e, nu: (j, 0, 0)),
                ids(lambda j, e, nu: (j + 1, 0, 0)),
                pl.BlockSpec(memory_space=pl.ANY),
                pl.BlockSpec((None, D, 2 * D_FF), lambda j, e, nu: (e[j], 0, 0)),
                pl.BlockSpec((None, 1, 2 * D_FF), lambda j, e, nu: (e[j], 0, 0)),
                pl.BlockSpec((None, D_FF, D), lambda j, e, nu: (e[j], 0, 0)),
                pl.BlockSpec((None, 1, D), lambda j, e, nu: (e[j], 0, 0)),
            ],
            out_specs=pl.BlockSpec(memory_space=pl.ANY),
            scratch_shapes=[pltpu.VMEM((2, te * ROW_SUBLANES, LANES), F32),
                            pltpu.VMEM((3, te * ROW_SUBLANES, LANES), F32),
                            pltpu.SemaphoreType.DMA((2,)), pltpu.SemaphoreType.DMA((3,))],
        ),
        out_shape=jax.ShapeDtypeStruct(((n_tokens * TOP_K + 2 * te) * ROW_SUBLANES, LANES), F32),
        compiler_params=_params("arbitrary"),
        name="moe",
    )(tile_expert, n_used.reshape(1), gid, gid, sid_prev, sid_prev, x1r,
      w_gate_up.astype(BF16), b_gate_up.reshape(N_EXPERTS, 1, 2 * D_FF).astype(F32),
      w_down.astype(BF16), b_down.reshape(N_EXPERTS, 1, D).astype(F32))


def _combine_kernel(x1_ref, x1b_ref, p_ref, wt_ref, wpg_ref, wpp_ref, g_ref, b_ref, y0_ref, y1_ref, y2_ref, y3_ref,
                    x2_ref, x2b_ref, *, tm):
    ple = jax.nn.sigmoid(_dot(x1b_ref[...], wpg_ref[...])) * _dot(p_ref[...].astype(BF16), wpp_ref[...])
    pieces = []
    for s in range(ROW_SUBLANES):
        acc = None
        for j, y_ref in enumerate((y0_ref, y1_ref, y2_ref, y3_ref)):
            term = wt_ref[:, j:j + 1] * y_ref[pl.ds(s, tm, stride=ROW_SUBLANES), :]
            acc = term if acc is None else acc + term
        pieces.append(acc)
    ffn = jnp.concatenate(pieces, axis=1)
    x2 = _layer_norm(DEEPNORM_ALPHA * x1_ref[...] + ffn + ple, g_ref[...], b_ref[...])
    x2_ref[...] = x2
    x2b_ref[...] = x2.astype(BF16)


def _combine(x1, x1b, p, wt, w_ple_gate, w_ple_proj, ln_g, ln_b, y4):
    T, D = x1.shape
    tm = min(WIDE_TILE, T)
    n_tt = T // tm
    weights = [w_ple_gate.astype(BF16), w_ple_proj.astype(BF16),
               ln_g.reshape(1, D).astype(F32), ln_b.reshape(1, D).astype(F32)]

    def choice(j):
        return pl.BlockSpec((tm * ROW_SUBLANES, LANES), lambda i: (j * n_tt + i, 0))

    return pl.pallas_call(
        functools.partial(_combine_kernel, tm=tm),
        grid=(n_tt,),
        in_specs=[_rows(tm, D), _rows(tm, D), _rows(tm, PLE_DIM), _rows(tm, TOP_K)]
                 + [_resident(w.shape) for w in weights] + [choice(j) for j in range(TOP_K)],
        out_specs=[_rows(tm, D), _rows(tm, D)],
        out_shape=[jax.ShapeDtypeStruct((T, D), F32), jax.ShapeDtypeStruct((T, D), BF16)],
        compiler_params=_params("parallel"),
        name="combine",
    )(x1, x1b, p, wt, *weights, y4, y4, y4, y4)


def _layer(x, xb, p, batch, seq, w):
    T, D = x.shape
    gu, vn, q, k, val, rs, la, sga, sgb = _inproj(xb, w["w_in"], w["w_gate_a"], w["b_gate_a"],
                                                  w["sgu_ln_g"], w["sgu_ln_b"])
    o = _gla(q, k, val, la, rs, w["gla_norm_g"], batch, seq)
    x1, x1b, x1r = _merge(x, gu, vn, o, sga, sgb, w["sgu_w"], w["sgu_b"], w["w_branch_a"], w["w_branch_b"],
                          w["w_out"], w["ln1_g"], w["ln1_b"])
    wt, counts, prefix, code = _router(x1, x1b, w["w_router"], w["b_router"])
    te = EXPERT_TILE
    n_tiles = (T * TOP_K) // te + N_EXPERTS
    tile_expert, rank0, lo, hi, n_used = _tile_plan(counts[0, :N_EXPERTS], prefix[:, 0, :N_EXPERTS], n_tiles, te)
    gid, sid = _invmap(tile_expert, rank0, lo, hi, code, n_tiles, te, T)
    y4 = _moe(tile_expert, n_used, gid, sid, x1r, w["w_gate_up"], w["b_gate_up"], w["w_down"], w["b_down"], T)
    return _combine(x1, x1b, p, wt, w["w_ple_gate"], w["w_ple_proj"], w["ln2_g"], w["ln2_b"], y4)


_WEIGHT_NAMES = ("w_in", "w_gate_a", "b_gate_a", "sgu_ln_g", "sgu_ln_b", "sgu_w", "sgu_b", "gla_norm_g",
                 "w_branch_a", "w_branch_b", "w_out", "ln1_g", "ln1_b", "w_router", "b_router", "w_gate_up",
                 "b_gate_up", "w_down", "b_down", "w_ple_proj", "w_ple_gate", "ln2_g", "ln2_b")


def kernel(x, p, w_in, w_gate_a, b_gate_a, sgu_ln_g, sgu_ln_b, sgu_w, sgu_b, gla_norm_g, w_branch_a, w_branch_b,
           w_out, ln1_g, ln1_b, w_router, b_router, w_gate_up, b_gate_up, w_down, b_down, w_ple_proj, w_ple_gate,
           ln2_g, ln2_b):
    stacked = dict(zip(_WEIGHT_NAMES, (
        w_in, w_gate_a, b_gate_a, sgu_ln_g, sgu_ln_b, sgu_w, sgu_b, gla_norm_g, w_branch_a, w_branch_b, w_out,
        ln1_g, ln1_b, w_router, b_router, w_gate_up, b_gate_up, w_down, b_down, w_ple_proj, w_ple_gate,
        ln2_g, ln2_b)))
    batch, seq, d = x.shape
    depth = w_in.shape[0]
    xf = x.reshape(batch * seq, d)
    xb = xf.astype(BF16)
    for i in range(depth):
        w = {name: arr[i] for name, arr in stacked.items()}
        xf, xb = _layer(xf, xb, p[i].reshape(batch * seq, PLE_DIM), batch, seq, w)
    return xf.reshape(batch, seq, d)
```

```python
import functools

import jax
import jax.numpy as jnp
from jax import lax
from jax.experimental import pallas as pl
from jax.experimental.pallas import tpu as pltpu

D_MODEL = 1024
DEPTH = 4
CHUNK = 64
SGU_WINDOW = 128
SGU_GROUPS = 8
GLA_HEADS = 4
GLA_DK = 128
GLA_DV = 256
GLA_GATE_RANK = 16
GLA_GATE_NORM = 16.0
N_EXPERTS = 32
TOP_K = 4
D_FF = 1024
SWIGLU_LIMIT = 7.0
SWIGLU_ALPHA = 1.702
PLE_DIM = 256
LN_EPS = 1e-5
DEEPNORM_ALPHA = (2 * DEPTH) ** 0.25
HK = GLA_HEADS * GLA_DK
HV = GLA_HEADS * GLA_DV
IN_SIZES = (D_MODEL, D_MODEL, HK, HK, HV, HV, GLA_GATE_RANK, D_MODEL, D_MODEL)

LANES = 128
ROW_SUBLANES = 8
EXPERT_LANES = 128
ROW_TILE = 256
WIDE_TILE = 512
INPROJ_TILE = WIDE_TILE
GLA_TILE = 512
EXPERT_TILE = 256
DMA_PRIORITIES = 2
INVMAP_UNROLL = 5
INVMAP_TILES = 8
VMEM_LIMIT = 56 * 1024 * 1024

F32 = jnp.float32
BF16 = jnp.bfloat16


def _dot(a, b, **kw):
    return jnp.dot(a, b, preferred_element_type=F32, **kw)


def _layer_norm(y, g, b):
    mu = jnp.mean(y, axis=-1, keepdims=True)
    d = y - mu
    var = jnp.mean(d * d, axis=-1, keepdims=True)
    return d * lax.rsqrt(var + LN_EPS) * g + b


def _params(*sem):
    return pltpu.CompilerParams(dimension_semantics=sem, vmem_limit_bytes=VMEM_LIMIT)


def _rows(tm, n):
    return pl.BlockSpec((tm, n), lambda i: (i, 0))


def _full(shape):
    nd = len(shape)
    return pl.BlockSpec(shape, lambda i: (0,) * nd)


def _resident(shape):
    nd = len(shape)
    return pl.BlockSpec(shape, lambda i: (0,) * nd, pipeline_mode=pl.Buffered(1))


def _inproj_kernel(x_ref, wmain_ref, wglr_ref, wgab_ref, wgate_ref, bgate_ref, lng_ref, lnb_ref,
                   gu_ref, vn_ref, q_ref, k_ref, val_ref, rs_ref, la_ref, sga_ref, sgb_ref):
    x = x_ref[...].astype(BF16)
    cols = []
    o = 0
    for n in IN_SIZES[:6]:
        cols.append(slice(o, o + n))
        o += n
    cu, cv, cq, ck, cval, cr = cols
    gu_ref[...] = jax.nn.gelu(_dot(x, wmain_ref[:, cu])).astype(BF16)
    v = jax.nn.gelu(_dot(x, wmain_ref[:, cv]))
    vn_ref[...] = _layer_norm(v, lng_ref[...], lnb_ref[...]).astype(BF16)
    q_ref[...] = _dot(x, wmain_ref[:, cq]).astype(BF16)
    k_ref[...] = _dot(x, wmain_ref[:, ck]).astype(BF16)
    val_ref[...] = _dot(x, wmain_ref[:, cval]).astype(BF16)
    r = _dot(x, wmain_ref[:, cr])
    rs_ref[...] = (r * jax.nn.sigmoid(r)).astype(BF16)
    glr = _dot(x, wglr_ref[...])
    z = _dot(glr, wgate_ref[...], precision=lax.Precision.HIGHEST) + bgate_ref[...]
    la_ref[...] = (jnp.minimum(z, 0.0) - jnp.log(1.0 + jnp.exp(-jnp.abs(z)))) * (1.0 / GLA_GATE_NORM)
    sga_ref[...] = jax.nn.sigmoid(_dot(x, wgab_ref[:, :D_MODEL])).astype(BF16)
    sgb_ref[...] = jax.nn.sigmoid(_dot(x, wgab_ref[:, D_MODEL:])).astype(BF16)


def _inproj(x, w_in, w_gate_a, b_gate_a, ln_g, ln_b):
    T, D = x.shape
    tm = min(INPROJ_TILE, T)
    n_main = sum(IN_SIZES[:6])
    n_glr = n_main + GLA_GATE_RANK
    weights = [w_in[:, :n_main].astype(BF16), w_in[:, n_main:n_glr].astype(BF16), w_in[:, n_glr:].astype(BF16),
               w_gate_a.astype(F32), b_gate_a.reshape(1, HK).astype(F32),
               ln_g.reshape(1, D).astype(F32), ln_b.reshape(1, D).astype(F32)]
    out_shapes = [
        jax.ShapeDtypeStruct((T, D), BF16),
        jax.ShapeDtypeStruct((T, D), BF16),
        jax.ShapeDtypeStruct((T, HK), BF16),
        jax.ShapeDtypeStruct((T, HK), BF16),
        jax.ShapeDtypeStruct((T, HV), BF16),
        jax.ShapeDtypeStruct((T, HV), BF16),
        jax.ShapeDtypeStruct((T, HK), F32),
        jax.ShapeDtypeStruct((T, D), BF16),
        jax.ShapeDtypeStruct((T, D), BF16),
    ]
    return pl.pallas_call(
        _inproj_kernel,
        grid=(T // tm,),
        in_specs=[_rows(tm, D)] + [_resident(w.shape) for w in weights],
        out_specs=[_rows(tm, s.shape[1]) for s in out_shapes],
        out_shape=out_shapes,
        compiler_params=_params("parallel"),
        name="inproj",
    )(x, *weights)


def _gla_kernel(q_ref, k_ref, v_ref, la_ref, rs_ref, g_ref, o_ref, st_ref, *, n_chunks):
    @pl.when(pl.program_id(1) == 0)
    def _():
        st_ref[...] = jnp.zeros_like(st_ref)

    ts = n_chunks * CHUNK
    row = lax.broadcasted_iota(jnp.int32, (CHUNK, CHUNK), 0)
    col = lax.broadcasted_iota(jnp.int32, (CHUNK, CHUNK), 1)
    causal = (row >= col)[None]
    nt = (((1,), (1,)), ((), ()))
    tn = (((0,), (0,)), ((), ()))
    bnt = (((2,), (2,)), ((0,), (0,)))
    bnn = (((2,), (1,)), ((0,), (0,)))

    pos = lax.broadcasted_iota(jnp.int32, (ts, 1), 0) % CHUNK
    b = la_ref[...]
    step = 1
    while step < CHUNK:
        b = b + jnp.where(pos >= step, pltpu.roll(b, step, axis=0), 0.0)
        step *= 2
    b = b.reshape(n_chunks, CHUNK, HK)
    b_end = b[:, CHUNK - 1:CHUNK, :]
    q = q_ref[...].astype(F32).reshape(n_chunks, CHUNK, HK) * (GLA_DK ** -0.5)
    k = k_ref[...].astype(F32).reshape(n_chunks, CHUNK, HK)
    qt_all = (q * jnp.exp(b)).astype(BF16)
    kt_all = (k * jnp.exp(-b)).astype(BF16)
    k_end_all = (k * jnp.exp(b_end - b)).astype(BF16)
    decay = jnp.exp(b_end)

    for h in range(GLA_HEADS):
        kc = slice(h * GLA_DK, (h + 1) * GLA_DK)
        vc = slice(h * GLA_DV, (h + 1) * GLA_DV)
        qt = qt_all[:, :, kc]
        vh = v_ref[:, vc].reshape(n_chunks, CHUNK, GLA_DV)
        att = lax.dot_general(qt, kt_all[:, :, kc], bnt, preferred_element_type=F32)
        att = jnp.where(causal, att, 0.0).astype(BF16)
        o_intra = lax.dot_general(att, vh, bnn, preferred_element_type=F32)
        st = st_ref[h]
        outs = []
        for c in range(n_chunks):
            outs.append(o_intra[c] + lax.dot_general(qt[c], st.astype(BF16), nt, preferred_element_type=F32))
            kv = lax.dot_general(vh[c], k_end_all[c, :, kc], tn, preferred_element_type=F32)
            st = decay[c, :, kc] * st + kv
        st_ref[h] = st
        o = jnp.concatenate(outs, axis=0)
        o = o * lax.rsqrt(jnp.mean(o * o, axis=-1, keepdims=True) + LN_EPS)
        o = o * g_ref[:, vc] * rs_ref[:, vc].astype(F32)
        o_ref[:, vc] = o.astype(BF16)


def _gla(q, k, v, la, rs, norm_g, batch, seq):
    T = q.shape[0]
    ts = min(GLA_TILE, seq)
    ns = seq // ts

    def rows(n):
        return pl.BlockSpec((ts, n), lambda b, s: (b * ns + s, 0))

    return pl.pallas_call(
        functools.partial(_gla_kernel, n_chunks=ts // CHUNK),
        grid=(batch, ns),
        in_specs=[rows(HK), rows(HK), rows(HV), rows(HK), rows(HV),
                  pl.BlockSpec((1, HV), lambda b, s: (0, 0))],
        out_specs=rows(HV),
        out_shape=jax.ShapeDtypeStruct((T, HV), BF16),
        scratch_shapes=[pltpu.VMEM((GLA_HEADS, GLA_DV, GLA_DK), F32)],
        compiler_params=_params("parallel", "arbitrary"),
        name="gla",
    )(q, k, v, la, rs, norm_g.reshape(1, HV).astype(F32))


def _merge_kernel(x_ref, gu_ref, vn_ref, o_ref, sga_ref, sgb_ref, ws_ref, bs_ref, wba_ref, wbb_ref, wout_ref,
                  g_ref, b_ref, x1_ref, x1b_ref, x1r_ref, a_ref, *, n_win, tm):
    row = lax.broadcasted_iota(jnp.int32, (SGU_WINDOW, SGU_WINDOW), 0) // CHUNK
    col = lax.broadcasted_iota(jnp.int32, (SGU_WINDOW, SGU_WINDOW), 1) // CHUNK
    keep = row >= col
    gd = D_MODEL // SGU_GROUPS
    for g in range(SGU_GROUPS):
        wm = jnp.where(keep, ws_ref[g], 0.0).astype(BF16)
        bias = bs_ref[:, g:g + 1]
        cs = slice(g * gd, (g + 1) * gd)
        wins = [slice(w * SGU_WINDOW, (w + 1) * SGU_WINDOW) for w in range(n_win)]
        s_all = _dot(wm, jnp.concatenate([vn_ref[rs, cs] for rs in wins], axis=1))
        for w, rs in enumerate(wins):
            s = s_all[:, w * gd:(w + 1) * gd] + bias
            a_ref[rs, cs] = (gu_ref[rs, cs].astype(F32) * s).astype(BF16)
    h = (sga_ref[...].astype(F32) * _dot(a_ref[...], wba_ref[...])
         + sgb_ref[...].astype(F32) * _dot(o_ref[...], wbb_ref[...]))
    mix = _dot(h.astype(BF16), wout_ref[...])
    x1 = _layer_norm(DEEPNORM_ALPHA * x_ref[...] + mix, g_ref[...], b_ref[...])
    x1_ref[...] = x1
    x1b_ref[...] = x1.astype(BF16)
    for s in range(ROW_SUBLANES):
        x1r_ref[pl.ds(s, tm, stride=ROW_SUBLANES), :] = x1[:, s * LANES:(s + 1) * LANES]


def _merge(x, gu, vn, o, sga, sgb, sgu_w, sgu_b, w_ba, w_bb, w_out, ln_g, ln_b):
    T, D = x.shape
    tm = min(WIDE_TILE, T)
    weights = [sgu_w.astype(F32), sgu_b.T.astype(F32), w_ba.astype(BF16), w_bb.astype(BF16), w_out.astype(BF16),
               ln_g.reshape(1, D).astype(F32), ln_b.reshape(1, D).astype(F32)]
    return pl.pallas_call(
        functools.partial(_merge_kernel, n_win=tm // SGU_WINDOW, tm=tm),
        grid=(T // tm,),
        in_specs=[_rows(tm, D)] * 6 + [_resident(w.shape) for w in weights],
        out_specs=[_rows(tm, D), _rows(tm, D), _rows(tm * ROW_SUBLANES, LANES)],
        out_shape=[jax.ShapeDtypeStruct((T, D), F32), jax.ShapeDtypeStruct((T, D), BF16),
                   jax.ShapeDtypeStruct((T * ROW_SUBLANES, LANES), F32)],
        scratch_shapes=[pltpu.VMEM((tm, D), BF16)],
        compiler_params=_params("parallel"),
        name="merge",
    )(x, gu, vn, o, sga, sgb, *weights)


def _router_kernel(x_ref, xb_ref, wr_ref, br_ref, wt_ref, cnt_ref, pre_ref, code_ref, carry_ref, *, tm):
    @pl.when(pl.program_id(0) == 0)
    def _():
        carry_ref[...] = jnp.zeros_like(carry_ref)

    x_hi = xb_ref[...]
    x_lo = (x_ref[...] - x_hi.astype(F32)).astype(BF16)
    logits = (_dot(x_hi, wr_ref[0]) + _dot(x_lo, wr_ref[0]) + _dot(x_hi, wr_ref[1])) + br_ref[...]
    lane = lax.broadcasted_iota(jnp.int32, (tm, EXPERT_LANES), 1).astype(F32)
    work = logits
    hots, vals = [], []
    for _ in range(TOP_K):
        m = jnp.max(work, axis=-1, keepdims=True)
        first = jnp.min(jnp.where(work == m, lane, float(EXPERT_LANES)), axis=-1, keepdims=True)
        hot = lane == first
        hots.append(hot)
        vals.append(m)
        work = jnp.where(hot, -jnp.inf, work)
    exps = [jnp.exp(v - vals[0]) for v in vals]
    denom = exps[0] + exps[1] + exps[2] + exps[3]
    for j in range(TOP_K):
        wt_ref[:, j:j + 1] = exps[j] / denom

    chosen = hots[0] | hots[1] | hots[2] | hots[3]
    mask = chosen.astype(F32)
    which = hots[1].astype(F32) + 2.0 * hots[2].astype(F32) + 3.0 * hots[3].astype(F32)
    r = lax.broadcasted_iota(jnp.int32, (tm, tm), 0)
    c = lax.broadcasted_iota(jnp.int32, (tm, tm), 1)
    below = (r > c).astype(BF16)
    rank = _dot(below, mask.astype(BF16)) + carry_ref[...]
    pre_ref[0] = carry_ref[...]
    carry_ref[...] += jnp.sum(mask, axis=0, keepdims=True)
    cnt_ref[...] = carry_ref[...]
    code = jnp.where(chosen, TOP_K * rank + which, -float(TOP_K))
    code_ref[0] = code.T[:N_EXPERTS, :].astype(jnp.int32)


def _router(x1, x1b, w_router, b_router):
    T, D = x1.shape
    tm = min(ROW_TILE, T)
    n_tt = T // tm
    pad = EXPERT_LANES - N_EXPERTS
    wr = jnp.pad(w_router.astype(F32), ((0, 0), (0, pad)))
    wr_hi = wr.astype(BF16)
    wr = jnp.stack([wr_hi, (wr - wr_hi.astype(F32)).astype(BF16)])
    br = jnp.pad(b_router.reshape(1, N_EXPERTS).astype(F32), ((0, 0), (0, pad)), constant_values=-1e30)
    return pl.pallas_call(
        functools.partial(_router_kernel, tm=tm),
        grid=(n_tt,),
        in_specs=[_rows(tm, D), _rows(tm, D), _full((2, D, EXPERT_LANES)), _full((1, EXPERT_LANES))],
        out_specs=[_rows(tm, TOP_K), _full((1, EXPERT_LANES)),
                   pl.BlockSpec((1, 1, EXPERT_LANES), lambda i: (i, 0, 0)),
                   pl.BlockSpec((1, N_EXPERTS, tm), lambda i: (i, 0, 0))],
        out_shape=[jax.ShapeDtypeStruct((T, TOP_K), F32), jax.ShapeDtypeStruct((1, EXPERT_LANES), F32),
                   jax.ShapeDtypeStruct((n_tt, 1, EXPERT_LANES), F32),
                   jax.ShapeDtypeStruct((n_tt, N_EXPERTS, tm), jnp.int32)],
        scratch_shapes=[pltpu.VMEM((1, EXPERT_LANES), F32)],
        compiler_params=_params("arbitrary"),
        name="router",
    )(x1, x1b, wr, br)


def _tile_plan(counts, prefix, n_tiles, te):
    counts = counts.astype(jnp.int32)
    prefix = prefix.astype(jnp.int32)
    padded = ((counts + te - 1) // te) * te
    ends = jnp.cumsum(padded)
    offsets = ends - padded
    n_used = ends[-1] // te
    tile = jnp.arange(n_tiles, dtype=jnp.int32)
    used = tile < n_used
    start = jnp.minimum(tile, n_used - 1) * te
    expert = jnp.sum((ends[None, :] <= start[:, None]).astype(jnp.int32), axis=1)
    rank0 = start - offsets[expert]
    before = prefix[:, expert].T
    after = jnp.concatenate([prefix[1:], counts[None, :]], axis=0)[:, expert].T
    lo = jnp.sum((after <= rank0[:, None]).astype(jnp.int32), axis=1)
    hi = jnp.sum((before < (rank0 + te)[:, None]).astype(jnp.int32), axis=1)
    lo = jnp.where(used, lo, 0)
    hi = jnp.where(used, hi, 0)
    return expert, rank0, lo, hi, n_used


def _invmap_kernel(te_ref, r0_ref, lo_ref, hi_ref, code_ref, gid_ref, sid_ref, *, tm, te, n_tokens):
    for t in range(INVMAP_TILES):
        _invmap_tile(pl.program_id(0) * INVMAP_TILES + t, t, te_ref, r0_ref, lo_ref, hi_ref, code_ref,
                     gid_ref, sid_ref, tm=tm, te=te, n_tokens=n_tokens)


def _invmap_tile(j, out_row, te_ref, r0_ref, lo_ref, hi_ref, code_ref, gid_ref, sid_ref, *, tm, te, n_tokens):
    e = te_ref[j]
    want = r0_ref[j] + lax.broadcasted_iota(jnp.int32, (te, 1), 0)
    tloc = lax.broadcasted_iota(jnp.int32, (1, tm), 1)
    lo = lo_ref[j]
    hi = hi_ref[j]
    n_tt = code_ref.shape[0]

    def body(g, acc):
        for u in range(INVMAP_UNROLL):
            i = lo + g * INVMAP_UNROLL + u
            ic = jnp.minimum(i, n_tt - 1)
            code = code_ref[ic, pl.ds(e, 1), :]
            rank = jnp.where(i < hi, code >> 2, -1)
            tag = ((ic * tm + tloc) * TOP_K + (code & (TOP_K - 1)) + 1).astype(F32)
            sel = jnp.where(rank == want, tag, 0.0)
            for c in range(tm // LANES):
                acc = acc + sel[:, c * LANES:(c + 1) * LANES]
        return acc

    n_groups = (hi - lo + INVMAP_UNROLL - 1) // INVMAP_UNROLL
    acc = lax.fori_loop(0, n_groups, body, jnp.zeros((te, LANES), F32))
    found = jnp.sum(acc, axis=1, keepdims=True).astype(jnp.int32) - 1
    valid = found >= 0
    tok = found >> 2
    spare = n_tokens * TOP_K + (j % 2) * te + lax.broadcasted_iota(jnp.int32, (te, 1), 0)
    rows = pl.ds(out_row * te, te)
    gid_ref[rows, :] = jnp.where(valid, tok, 0)
    sid_ref[rows, :] = jnp.where(valid, (found & (TOP_K - 1)) * n_tokens + tok, spare)


def _invmap(tile_expert, rank0, lo, hi, code, n_tiles, te, n_tokens):
    n_tt, _, tm = code.shape
    spec = pl.BlockSpec((INVMAP_TILES * te, 1), lambda j, *_: (j, 0))
    gid, sid = pl.pallas_call(
        functools.partial(_invmap_kernel, tm=tm, te=te, n_tokens=n_tokens),
        grid_spec=pltpu.PrefetchScalarGridSpec(
            num_scalar_prefetch=4, grid=(n_tiles // INVMAP_TILES,),
            in_specs=[pl.BlockSpec((n_tt, N_EXPERTS, tm), lambda j, *_: (0, 0, 0))],
            out_specs=[spec, spec]),
        out_shape=[jax.ShapeDtypeStruct((n_tiles * te, 1), jnp.int32)] * 2,
        compiler_params=_params("parallel"),
        name="invmap",
    )(tile_expert, rank0, lo, hi, code)
    return gid.reshape(n_tiles, 1, te), sid.reshape(n_tiles, 1, te)


def _moe_kernel(te_ref, nu_ref, gid0_ref, gidn_ref, sidp_ref, sidc_ref, x1r_ref, wgu_ref, bgu_ref, wd_ref, bd_ref,
                y4_ref, xg, yb, gsem, ssem, *, te):
    j = pl.program_id(0)
    n_used = nu_ref[0]
    slot = j % 2
    other = 1 - slot
    out_slot = j % 3
    prev_slot = (j + 2) % 3
    rows = te * ROW_SUBLANES
    n_rows = y4_ref.shape[0] - 2 * rows

    def gather_row(ids_ref, par, r):
        src = pl.multiple_of(ids_ref[0, 0, r] * ROW_SUBLANES, ROW_SUBLANES)
        return pltpu.make_async_copy(x1r_ref.at[pl.ds(src, ROW_SUBLANES), :],
                                     xg.at[par, pl.ds(r * ROW_SUBLANES, ROW_SUBLANES), :], gsem.at[par])

    def scatter_row(ids_ref, par, r):
        dst = pl.multiple_of(ids_ref[0, 0, r] * ROW_SUBLANES, ROW_SUBLANES)
        return pltpu.make_async_copy(yb.at[par, pl.ds(r * ROW_SUBLANES, ROW_SUBLANES), :],
                                     y4_ref.at[pl.ds(dst, ROW_SUBLANES), :], ssem.at[par])

    def wait_gather(par):
        pltpu.make_async_copy(x1r_ref.at[pl.ds(0, rows), :], xg.at[par], gsem.at[par]).wait()

    def wait_scatter(par):
        pltpu.make_async_copy(yb.at[par], y4_ref.at[pl.ds(0, rows), :], ssem.at[par]).wait()

    def step():
        @pl.when(j == 0)
        def _():
            yb[...] = jnp.zeros_like(yb)
            for region in range(2):
                spare = pltpu.make_async_copy(yb.at[region], y4_ref.at[pl.ds(n_rows + region * rows, rows), :],
                                              ssem.at[region])
                spare.start()
                spare.wait()

            def first(r, c):
                gather_row(gid0_ref, 0, r).start()
                return c

            lax.fori_loop(0, te, first, 0)

        @pl.when(j >= 2)
        def _():
            wait_scatter(out_slot)

        wait_gather(slot)

        x = jnp.concatenate([xg[slot, pl.ds(s, te, stride=ROW_SUBLANES), :].astype(BF16)
                             for s in range(ROW_SUBLANES)], axis=1)
        for r in range(te):
            gather_row(gidn_ref, other, r).start(priority=r % DMA_PRIORITIES)
        for r in range(te):
            scatter_row(sidp_ref, prev_slot, r).start(priority=r % DMA_PRIORITIES)
        hg = _dot(x, wgu_ref[:, :D_FF]) + bgu_ref[:, :D_FF]
        runtime_zero = jnp.minimum(te_ref[j], 0)
        probe = pltpu.bitcast(xg[slot + runtime_zero, 0:ROW_SUBLANES, :], jnp.int32) & runtime_zero
        plus_zero = pltpu.bitcast(probe, F32)[0:1, :]
        bias_up = bgu_ref[:, D_FF:] + jnp.concatenate([plus_zero] * (D_FF // LANES), axis=1)
        hu = _dot(x, wgu_ref[:, D_FF:]) + bias_up
        gate = jnp.minimum(hg, SWIGLU_LIMIT)
        up = jnp.clip(hu, -SWIGLU_LIMIT, SWIGLU_LIMIT)
        act = (up + 1.0) * gate * jax.nn.sigmoid(SWIGLU_ALPHA * gate)
        y = _dot(act.astype(BF16), wd_ref[...]) + bd_ref[...]
        for s in range(ROW_SUBLANES):
            yb[out_slot, pl.ds(s, te, stride=ROW_SUBLANES), :] = y[:, s * LANES:(s + 1) * LANES]

        @pl.when(j == n_used - 1)
        def _():
            wait_scatter((j + 1) % 3)
            wait_scatter(prev_slot)
            wait_gather(other)

            def last(r, c):
                scatter_row(sidc_ref, out_slot, r).start()
                return c

            lax.fori_loop(0, te, last, 0)
            wait_scatter(out_slot)

    pl.when(j < n_used)(step)


def _moe(tile_expert, n_used, gid, sid, x1r, w_gate_up, b_gate_up, w_down, b_down, n_tokens):
    n_tiles, _, te = gid.shape
    D = D_MODEL
    spare = (n_tokens * TOP_K + te + jnp.arange(te, dtype=jnp.int32)).reshape(1, 1, te)
    sid_prev = jnp.concatenate([spare, sid], axis=0)
    last = n_tiles - 1

    def ids(index):
        return pl.BlockSpec((1, 1, te), index, memory_space=pltpu.SMEM)

    return pl.pallas_call(
        functools.partial(_moe_kernel, te=te),
        grid_spec=pltpu.PrefetchScalarGridSpec(
            num_scalar_prefetch=2,
            grid=(n_tiles,),
            in_specs=[
                ids(lambda j, e, nu: (0, 0, 0)),
                ids(lambda j, e, nu: (jnp.minimum(j + 1, last), 0, 0)),
                ids(lambda j, e, nu: (j, 0, 0)),
                ids(lambda j, e, nu: (j + 1, 0, 0)),
                pl.BlockSpec(memory_space=pl.ANY),
                pl.BlockSpec((None, D, 2 * D_FF), lambda j, e, nu: (e[j], 0, 0)),
                pl.BlockSpec((None, 1, 2 * D_FF), lambda j, e, nu: (e[j], 0, 0)),
                pl.BlockSpec((None, D_FF, D), lambda j, e, nu: (e[j], 0, 0)),
                pl.BlockSpec((None, 1, D), lambda j, e, nu: (e[j], 0, 0)),
            ],
            out_specs=pl.BlockSpec(memory_space=pl.ANY),
            scratch_shapes=[pltpu.VMEM((2, te * ROW_SUBLANES, LANES), F32),
                            pltpu.VMEM((3, te * ROW_SUBLANES, LANES), F32),
                            pltpu.SemaphoreType.DMA((2,)), pltpu.SemaphoreType.DMA((3,))],
        ),
        out_shape=jax.ShapeDtypeStruct(((n_tokens * TOP_K + 2 * te) * ROW_SUBLANES, LANES), F32),
        compiler_params=_params("arbitrary"),
        name="moe",
    )(tile_expert, n_used.reshape(1), gid, gid, sid_prev, sid_prev, x1r,
      w_gate_up.astype(BF16), b_gate_up.reshape(N_EXPERTS, 1, 2 * D_FF).astype(F32),
      w_down.astype(BF16), b_down.reshape(N_EXPERTS, 1, D).astype(F32))


def _combine_kernel(x1_ref, p_ref, wt_ref, wpg_ref, wpp_ref, g_ref, b_ref, y0_ref, y1_ref, y2_ref, y3_ref,
                    x2_ref, *, tm):
    ple = (jax.nn.sigmoid(_dot(x1_ref[...].astype(BF16), wpg_ref[...]))
           * _dot(p_ref[...].astype(BF16), wpp_ref[...]))
    pieces = []
    for s in range(ROW_SUBLANES):
        acc = None
        for j, y_ref in enumerate((y0_ref, y1_ref, y2_ref, y3_ref)):
            term = wt_ref[:, j:j + 1] * y_ref[pl.ds(s, tm, stride=ROW_SUBLANES), :]
            acc = term if acc is None else acc + term
        pieces.append(acc)
    ffn = jnp.concatenate(pieces, axis=1)
    x2 = _layer_norm(DEEPNORM_ALPHA * x1_ref[...] + ffn + ple, g_ref[...], b_ref[...])
    x2_ref[...] = x2


def _combine(x1, p, wt, w_ple_gate, w_ple_proj, ln_g, ln_b, y4):
    T, D = x1.shape
    tm = min(WIDE_TILE, T)
    n_tt = T // tm
    weights = [w_ple_gate.astype(BF16), w_ple_proj.astype(BF16),
               ln_g.reshape(1, D).astype(F32), ln_b.reshape(1, D).astype(F32)]

    def choice(j):
        return pl.BlockSpec((tm * ROW_SUBLANES, LANES), lambda i: (j * n_tt + i, 0))

    return pl.pallas_call(
        functools.partial(_combine_kernel, tm=tm),
        grid=(n_tt,),
        in_specs=[_rows(tm, D), _rows(tm, PLE_DIM), _rows(tm, TOP_K)]
                 + [_resident(w.shape) for w in weights] + [choice(j) for j in range(TOP_K)],
        out_specs=_rows(tm, D),
        out_shape=jax.ShapeDtypeStruct((T, D), F32),
        compiler_params=_params("parallel"),
        name="combine",
    )(x1, p, wt, *weights, y4, y4, y4, y4)


def _layer(x, p, batch, seq, w):
    T, D = x.shape
    gu, vn, q, k, val, rs, la, sga, sgb = _inproj(x, w["w_in"], w["w_gate_a"], w["b_gate_a"],
                                                  w["sgu_ln_g"], w["sgu_ln_b"])
    o = _gla(q, k, val, la, rs, w["gla_norm_g"], batch, seq)
    x1, x1b, x1r = _merge(x, gu, vn, o, sga, sgb, w["sgu_w"], w["sgu_b"], w["w_branch_a"], w["w_branch_b"],
                          w["w_out"], w["ln1_g"], w["ln1_b"])
    wt, counts, prefix, code = _router(x1, x1b, w["w_router"], w["b_router"])
    te = EXPERT_TILE
    n_tiles = (T * TOP_K) // te + N_EXPERTS
    tile_expert, rank0, lo, hi, n_used = _tile_plan(counts[0, :N_EXPERTS], prefix[:, 0, :N_EXPERTS], n_tiles, te)
    gid, sid = _invmap(tile_expert, rank0, lo, hi, code, n_tiles, te, T)
    y4 = _moe(tile_expert, n_used, gid, sid, x1r, w["w_gate_up"], w["b_gate_up"], w["w_down"], w["b_down"], T)
    return _combine(x1, p, wt, w["w_ple_gate"], w["w_ple_proj"], w["ln2_g"], w["ln2_b"], y4)


_WEIGHT_NAMES = ("w_in", "w_gate_a", "b_gate_a", "sgu_ln_g", "sgu_ln_b", "sgu_w", "sgu_b", "gla_norm_g",
                 "w_branch_a", "w_branch_b", "w_out", "ln1_g", "ln1_b", "w_router", "b_router", "w_gate_up",
                 "b_gate_up", "w_down", "b_down", "w_ple_proj", "w_ple_gate", "ln2_g", "ln2_b")


def kernel(x, p, w_in, w_gate_a, b_gate_a, sgu_ln_g, sgu_ln_b, sgu_w, sgu_b, gla_norm_g, w_branch_a, w_branch_b,
           w_out, ln1_g, ln1_b, w_router, b_router, w_gate_up, b_gate_up, w_down, b_down, w_ple_proj, w_ple_gate,
           ln2_g, ln2_b):
    stacked = dict(zip(_WEIGHT_NAMES, (
        w_in, w_gate_a, b_gate_a, sgu_ln_g, sgu_ln_b, sgu_w, sgu_b, gla_norm_g, w_branch_a, w_branch_b, w_out,
        ln1_g, ln1_b, w_router, b_router, w_gate_up, b_gate_up, w_down, b_down, w_ple_proj, w_ple_gate,
        ln2_g, ln2_b)))
    batch, seq, d = x.shape
    depth = w_in.shape[0]
    xf = x.reshape(batch * seq, d)
    for i in range(depth):
        w = {name: arr[i] for name, arr in stacked.items()}
        xf = _layer(xf, p[i].reshape(batch * seq, PLE_DIM), batch, seq, w)
    return xf.reshape(batch, seq, d)
```

```python
import functools

import jax
import jax.numpy as jnp
from jax import lax
from jax.experimental import pallas as pl
from jax.experimental.pallas import tpu as pltpu

D_MODEL = 1024
DEPTH = 4
CHUNK = 64
SGU_WINDOW = 128
SGU_GROUPS = 8
GLA_HEADS = 4
GLA_DK = 128
GLA_DV = 256
GLA_GATE_RANK = 16
GLA_GATE_NORM = 16.0
N_EXPERTS = 32
TOP_K = 4
D_FF = 1024
SWIGLU_LIMIT = 7.0
SWIGLU_ALPHA = 1.702
PLE_DIM = 256
LN_EPS = 1e-5
DEEPNORM_ALPHA = (2 * DEPTH) ** 0.25
HK = GLA_HEADS * GLA_DK
HV = GLA_HEADS * GLA_DV
IN_SIZES = (D_MODEL, D_MODEL, HK, HK, HV, HV, GLA_GATE_RANK, D_MODEL, D_MODEL)

LANES = 128
ROW_SUBLANES = 8
EXPERT_LANES = 128
ROW_TILE = 256
WIDE_TILE = 512
INPROJ_TILE = WIDE_TILE
GLA_TILE = 512
EXPERT_TILE = 256
DMA_PRIORITIES = 2
INVMAP_UNROLL = 5
INVMAP_TILES = 8
VMEM_LIMIT = 56 * 1024 * 1024

F32 = jnp.float32
BF16 = jnp.bfloat16


def _dot(a, b, **kw):
    return jnp.dot(a, b, preferred_element_type=F32, **kw)


def _layer_norm(y, g, b):
    mu = jnp.mean(y, axis=-1, keepdims=True)
    d = y - mu
    var = jnp.mean(d * d, axis=-1, keepdims=True)
    return d * lax.rsqrt(var + LN_EPS) * g + b


def _params(*sem):
    return pltpu.CompilerParams(dimension_semantics=sem, vmem_limit_bytes=VMEM_LIMIT)


def _rows(tm, n):
    return pl.BlockSpec((tm, n), lambda i: (i, 0))


def _full(shape):
    nd = len(shape)
    return pl.BlockSpec(shape, lambda i: (0,) * nd)


def _resident(shape):
    nd = len(shape)
    return pl.BlockSpec(shape, lambda i: (0,) * nd, pipeline_mode=pl.Buffered(1))


def _inproj_kernel(x_ref, wmain_ref, wglr_ref, wgab_ref, wgate_ref, bgate_ref, lng_ref, lnb_ref,
                   gu_ref, vn_ref, q_ref, k_ref, val_ref, rs_ref, la_ref, sga_ref, sgb_ref):
    x = x_ref[...].astype(BF16)
    cols = []
    o = 0
    for n in IN_SIZES[:6]:
        cols.append(slice(o, o + n))
        o += n
    cu, cv, cq, ck, cval, cr = cols
    gu_ref[...] = jax.nn.gelu(_dot(x, wmain_ref[:, cu])).astype(BF16)
    v = jax.nn.gelu(_dot(x, wmain_ref[:, cv]))
    vn_ref[...] = _layer_norm(v, lng_ref[...], lnb_ref[...]).astype(BF16)
    q_ref[...] = _dot(x, wmain_ref[:, cq]).astype(BF16)
    k_ref[...] = _dot(x, wmain_ref[:, ck]).astype(BF16)
    val_ref[...] = _dot(x, wmain_ref[:, cval]).astype(BF16)
    r = _dot(x, wmain_ref[:, cr])
    rs_ref[...] = (r * jax.nn.sigmoid(r)).astype(BF16)
    glr = _dot(x, wglr_ref[...])
    z = _dot(glr, wgate_ref[...], precision=lax.Precision.HIGHEST) + bgate_ref[...]
    la_ref[...] = (jnp.minimum(z, 0.0) - jnp.log(1.0 + jnp.exp(-jnp.abs(z)))) * (1.0 / GLA_GATE_NORM)
    sga_ref[...] = jax.nn.sigmoid(_dot(x, wgab_ref[:, :D_MODEL])).astype(BF16)
    sgb_ref[...] = jax.nn.sigmoid(_dot(x, wgab_ref[:, D_MODEL:])).astype(BF16)


def _inproj(x, w_in, w_gate_a, b_gate_a, ln_g, ln_b):
    T, D = x.shape
    tm = min(INPROJ_TILE, T)
    n_main = sum(IN_SIZES[:6])
    n_glr = n_main + GLA_GATE_RANK
    weights = [w_in[:, :n_main].astype(BF16), w_in[:, n_main:n_glr].astype(BF16), w_in[:, n_glr:].astype(BF16),
               w_gate_a.astype(F32), b_gate_a.reshape(1, HK).astype(F32),
               ln_g.reshape(1, D).astype(F32), ln_b.reshape(1, D).astype(F32)]
    out_shapes = [
        jax.ShapeDtypeStruct((T, D), BF16),
        jax.ShapeDtypeStruct((T, D), BF16),
        jax.ShapeDtypeStruct((T, HK), BF16),
        jax.ShapeDtypeStruct((T, HK), BF16),
        jax.ShapeDtypeStruct((T, HV), BF16),
        jax.ShapeDtypeStruct((T, HV), BF16),
        jax.ShapeDtypeStruct((T, HK), F32),
        jax.ShapeDtypeStruct((T, D), BF16),
        jax.ShapeDtypeStruct((T, D), BF16),
    ]
    return pl.pallas_call(
        _inproj_kernel,
        grid=(T // tm,),
        in_specs=[_rows(tm, D)] + [_resident(w.shape) for w in weights],
        out_specs=[_rows(tm, s.shape[1]) for s in out_shapes],
        out_shape=out_shapes,
        compiler_params=_params("parallel"),
        name="inproj",
    )(x, *weights)


def _gla_kernel(q_ref, k_ref, v_ref, la_ref, rs_ref, g_ref, o_ref, st_ref, *, n_chunks):
    @pl.when(pl.program_id(1) == 0)
    def _():
        st_ref[...] = jnp.zeros_like(st_ref)

    ts = n_chunks * CHUNK
    row = lax.broadcasted_iota(jnp.int32, (CHUNK, CHUNK), 0)
    col = lax.broadcasted_iota(jnp.int32, (CHUNK, CHUNK), 1)
    causal = (row >= col)[None]
    nt = (((1,), (1,)), ((), ()))
    tn = (((0,), (0,)), ((), ()))
    bnt = (((2,), (2,)), ((0,), (0,)))
    bnn = (((2,), (1,)), ((0,), (0,)))

    pos = lax.broadcasted_iota(jnp.int32, (ts, 1), 0) % CHUNK
    b = la_ref[...]
    step = 1
    while step < CHUNK:
        b = b + jnp.where(pos >= step, pltpu.roll(b, step, axis=0), 0.0)
        step *= 2
    b = b.reshape(n_chunks, CHUNK, HK)
    b_end = b[:, CHUNK - 1:CHUNK, :]
    q = q_ref[...].astype(F32).reshape(n_chunks, CHUNK, HK) * (GLA_DK ** -0.5)
    k = k_ref[...].astype(F32).reshape(n_chunks, CHUNK, HK)
    qt_all = (q * jnp.exp(b)).astype(BF16)
    kt_all = (k * jnp.exp(-b)).astype(BF16)
    k_end_all = (k * jnp.exp(b_end - b)).astype(BF16)
    decay = jnp.exp(b_end)

    for h in range(GLA_HEADS):
        kc = slice(h * GLA_DK, (h + 1) * GLA_DK)
        vc = slice(h * GLA_DV, (h + 1) * GLA_DV)
        qt = qt_all[:, :, kc]
        vh = v_ref[:, vc].reshape(n_chunks, CHUNK, GLA_DV)
        att = lax.dot_general(qt, kt_all[:, :, kc], bnt, preferred_element_type=F32)
        att = jnp.where(causal, att, 0.0).astype(BF16)
        o_intra = lax.dot_general(att, vh, bnn, preferred_element_type=F32)
        st = st_ref[h]
        outs = []
        for c in range(n_chunks):
            outs.append(o_intra[c] + lax.dot_general(qt[c], st.astype(BF16), nt, preferred_element_type=F32))
            kv = lax.dot_general(vh[c], k_end_all[c, :, kc], tn, preferred_element_type=F32)
            st = decay[c, :, kc] * st + kv
        st_ref[h] = st
        o = jnp.concatenate(outs, axis=0)
        o = o * lax.rsqrt(jnp.mean(o * o, axis=-1, keepdims=True) + LN_EPS)
        o = o * g_ref[:, vc] * rs_ref[:, vc].astype(F32)
        o_ref[:, vc] = o.astype(BF16)


def _gla(q, k, v, la, rs, norm_g, batch, seq):
    T = q.shape[0]
    ts = min(GLA_TILE, seq)
    ns = seq // ts

    def rows(n):
        return pl.BlockSpec((ts, n), lambda b, s: (b * ns + s, 0))

    return pl.pallas_call(
        functools.partial(_gla_kernel, n_chunks=ts // CHUNK),
        grid=(batch, ns),
        in_specs=[rows(HK), rows(HK), rows(HV), rows(HK), rows(HV),
                  pl.BlockSpec((1, HV), lambda b, s: (0, 0))],
        out_specs=rows(HV),
        out_shape=jax.ShapeDtypeStruct((T, HV), BF16),
        scratch_shapes=[pltpu.VMEM((GLA_HEADS, GLA_DV, GLA_DK), F32)],
        compiler_params=_params("parallel", "arbitrary"),
        name="gla",
    )(q, k, v, la, rs, norm_g.reshape(1, HV).astype(F32))


def _merge_kernel(x_ref, gu_ref, vn_ref, o_ref, sga_ref, sgb_ref, ws_ref, bs_ref, wba_ref, wbb_ref, wout_ref,
                  g_ref, b_ref, x1_ref, x1b_ref, x1r_ref, a_ref, *, n_win, tm):
    row = lax.broadcasted_iota(jnp.int32, (SGU_WINDOW, SGU_WINDOW), 0) // CHUNK
    col = lax.broadcasted_iota(jnp.int32, (SGU_WINDOW, SGU_WINDOW), 1) // CHUNK
    keep = row >= col
    gd = D_MODEL // SGU_GROUPS
    for g in range(SGU_GROUPS):
        wm = jnp.where(keep, ws_ref[g], 0.0).astype(BF16)
        bias = bs_ref[:, g:g + 1]
        cs = slice(g * gd, (g + 1) * gd)
        wins = [slice(w * SGU_WINDOW, (w + 1) * SGU_WINDOW) for w in range(n_win)]
        s_all = _dot(wm, jnp.concatenate([vn_ref[rs, cs] for rs in wins], axis=1))
        for w, rs in enumerate(wins):
            s = s_all[:, w * gd:(w + 1) * gd] + bias
            a_ref[rs, cs] = (gu_ref[rs, cs].astype(F32) * s).astype(BF16)
    h = (sga_ref[...].astype(F32) * _dot(a_ref[...], wba_ref[...])
         + sgb_ref[...].astype(F32) * _dot(o_ref[...], wbb_ref[...]))
    mix = _dot(h.astype(BF16), wout_ref[...])
    x1 = _layer_norm(DEEPNORM_ALPHA * x_ref[...] + mix, g_ref[...], b_ref[...])
    x1_ref[...] = x1
    x1b_ref[...] = x1.astype(BF16)
    for s in range(ROW_SUBLANES):
        x1r_ref[pl.ds(s, tm, stride=ROW_SUBLANES), :] = x1[:, s * LANES:(s + 1) * LANES]


def _merge(x, gu, vn, o, sga, sgb, sgu_w, sgu_b, w_ba, w_bb, w_out, ln_g, ln_b):
    T, D = x.shape
    tm = min(WIDE_TILE, T)
    weights = [sgu_w.astype(F32), sgu_b.T.astype(F32), w_ba.astype(BF16), w_bb.astype(BF16), w_out.astype(BF16),
               ln_g.reshape(1, D).astype(F32), ln_b.reshape(1, D).astype(F32)]
    return pl.pallas_call(
        functools.partial(_merge_kernel, n_win=tm // SGU_WINDOW, tm=tm),
        grid=(T // tm,),
        in_specs=[_rows(tm, D)] * 6 + [_resident(w.shape) for w in weights],
        out_specs=[_rows(tm, D), _rows(tm, D), _rows(tm * ROW_SUBLANES, LANES)],
        out_shape=[jax.ShapeDtypeStruct((T, D), F32), jax.ShapeDtypeStruct((T, D), BF16),
                   jax.ShapeDtypeStruct((T * ROW_SUBLANES, LANES), F32)],
        scratch_shapes=[pltpu.VMEM((tm, D), BF16)],
        compiler_params=_params("parallel"),
        name="merge",
    )(x, gu, vn, o, sga, sgb, *weights)


def _router_kernel(x_ref, xb_ref, wr_ref, br_ref, wt_ref, cnt_ref, pre_ref, code_ref, carry_ref, *, tm):
    @pl.when(pl.program_id(0) == 0)
    def _():
        carry_ref[...] = jnp.zeros_like(carry_ref)

    x_hi = xb_ref[...]
    x_lo = (x_ref[...] - x_hi.astype(F32)).astype(BF16)
    logits = (_dot(x_hi, wr_ref[0]) + _dot(x_lo, wr_ref[0]) + _dot(x_hi, wr_ref[1])) + br_ref[...]
    lane = lax.broadcasted_iota(jnp.int32, (tm, EXPERT_LANES), 1).astype(F32)
    work = logits
    hots, vals = [], []
    for _ in range(TOP_K):
        m = jnp.max(work, axis=-1, keepdims=True)
        first = jnp.min(jnp.where(work == m, lane, float(EXPERT_LANES)), axis=-1, keepdims=True)
        hot = lane == first
        hots.append(hot)
        vals.append(m)
        work = jnp.where(hot, -jnp.inf, work)
    exps = [jnp.exp(v - vals[0]) for v in vals]
    denom = exps[0] + exps[1] + exps[2] + exps[3]
    for j in range(TOP_K):
        wt_ref[:, j:j + 1] = exps[j] / denom

    chosen = hots[0] | hots[1] | hots[2] | hots[3]
    mask = chosen.astype(F32)
    which = hots[1].astype(F32) + 2.0 * hots[2].astype(F32) + 3.0 * hots[3].astype(F32)
    r = lax.broadcasted_iota(jnp.int32, (tm, tm), 0)
    c = lax.broadcasted_iota(jnp.int32, (tm, tm), 1)
    below = (r > c).astype(BF16)
    rank = _dot(below, mask.astype(BF16)) + carry_ref[...]
    pre_ref[0] = carry_ref[...]
    carry_ref[...] += jnp.sum(mask, axis=0, keepdims=True)
    cnt_ref[...] = carry_ref[...]
    code = jnp.where(chosen, TOP_K * rank + which, -float(TOP_K))
    code_ref[0] = code.T[:N_EXPERTS, :].astype(jnp.int32)


def _router(x1, x1b, w_router, b_router):
    T, D = x1.shape
    tm = min(ROW_TILE, T)
    n_tt = T // tm
    pad = EXPERT_LANES - N_EXPERTS
    wr = jnp.pad(w_router.astype(F32), ((0, 0), (0, pad)))
    wr_hi = wr.astype(BF16)
    wr = jnp.stack([wr_hi, (wr - wr_hi.astype(F32)).astype(BF16)])
    br = jnp.pad(b_router.reshape(1, N_EXPERTS).astype(F32), ((0, 0), (0, pad)), constant_values=-1e30)
    return pl.pallas_call(
        functools.partial(_router_kernel, tm=tm),
        grid=(n_tt,),
        in_specs=[_rows(tm, D), _rows(tm, D), _full((2, D, EXPERT_LANES)), _full((1, EXPERT_LANES))],
        out_specs=[_rows(tm, TOP_K), _full((1, EXPERT_LANES)),
                   pl.BlockSpec((1, 1, EXPERT_LANES), lambda i: (i, 0, 0)),
                   pl.BlockSpec((1, N_EXPERTS, tm), lambda i: (i, 0, 0))],
        out_shape=[jax.ShapeDtypeStruct((T, TOP_K), F32), jax.ShapeDtypeStruct((1, EXPERT_LANES), F32),
                   jax.ShapeDtypeStruct((n_tt, 1, EXPERT_LANES), F32),
                   jax.ShapeDtypeStruct((n_tt, N_EXPERTS, tm), jnp.int32)],
        scratch_shapes=[pltpu.VMEM((1, EXPERT_LANES), F32)],
        compiler_params=_params("arbitrary"),
        name="router",
    )(x1, x1b, wr, br)


def _tile_plan(counts, prefix, n_tiles, te):
    counts = counts.astype(jnp.int32)
    prefix = prefix.astype(jnp.int32)
    padded = ((counts + te - 1) // te) * te
    ends = jnp.cumsum(padded)
    offsets = ends - padded
    n_used = ends[-1] // te
    tile = jnp.arange(n_tiles, dtype=jnp.int32)
    used = tile < n_used
    start = jnp.minimum(tile, n_used - 1) * te
    expert = jnp.sum((ends[None, :] <= start[:, None]).astype(jnp.int32), axis=1)
    rank0 = start - offsets[expert]
    before = prefix[:, expert].T
    after = jnp.concatenate([prefix[1:], counts[None, :]], axis=0)[:, expert].T
    lo = jnp.sum((after <= rank0[:, None]).astype(jnp.int32), axis=1)
    hi = jnp.sum((before < (rank0 + te)[:, None]).astype(jnp.int32), axis=1)
    lo = jnp.where(used, lo, 0)
    hi = jnp.where(used, hi, 0)
    return expert, rank0, lo, hi, n_used


def _invmap_kernel(te_ref, r0_ref, lo_ref, hi_ref, code_ref, gid_ref, sid_ref, *, tm, te, n_tokens):
    for t in range(INVMAP_TILES):
        _invmap_tile(pl.program_id(0) * INVMAP_TILES + t, t, te_ref, r0_ref, lo_ref, hi_ref, code_ref,
                     gid_ref, sid_ref, tm=tm, te=te, n_tokens=n_tokens)


def _invmap_tile(j, out_row, te_ref, r0_ref, lo_ref, hi_ref, code_ref, gid_ref, sid_ref, *, tm, te, n_tokens):
    e = te_ref[j]
    want = r0_ref[j] + lax.broadcasted_iota(jnp.int32, (te, 1), 0)
    tloc = lax.broadcasted_iota(jnp.int32, (1, tm), 1)
    lo = lo_ref[j]
    hi = hi_ref[j]
    n_tt = code_ref.shape[0]

    def body(g, acc):
        for u in range(INVMAP_UNROLL):
            i = lo + g * INVMAP_UNROLL + u
            ic = jnp.minimum(i, n_tt - 1)
            code = code_ref[ic, pl.ds(e, 1), :]
            rank = jnp.where(i < hi, code >> 2, -1)
            tag = ((ic * tm + tloc) * TOP_K + (code & (TOP_K - 1)) + 1).astype(F32)
            sel = jnp.where(rank == want, tag, 0.0)
            for c in range(tm // LANES):
                acc = acc + sel[:, c * LANES:(c + 1) * LANES]
        return acc

    n_groups = (hi - lo + INVMAP_UNROLL - 1) // INVMAP_UNROLL
    acc = lax.fori_loop(0, n_groups, body, jnp.zeros((te, LANES), F32))
    found = jnp.sum(acc, axis=1, keepdims=True).astype(jnp.int32) - 1
    valid = found >= 0
    tok = found >> 2
    spare = n_tokens * TOP_K + (j % 2) * te + lax.broadcasted_iota(jnp.int32, (te, 1), 0)
    rows = pl.ds(out_row * te, te)
    gid_ref[rows, :] = jnp.where(valid, tok, 0)
    sid_ref[rows, :] = jnp.where(valid, (found & (TOP_K - 1)) * n_tokens + tok, spare)


def _invmap(tile_expert, rank0, lo, hi, code, n_tiles, te, n_tokens):
    n_tt, _, tm = code.shape
    spec = pl.BlockSpec((INVMAP_TILES * te, 1), lambda j, *_: (j, 0))
    gid, sid = pl.pallas_call(
        functools.partial(_invmap_kernel, tm=tm, te=te, n_tokens=n_tokens),
        grid_spec=pltpu.PrefetchScalarGridSpec(
            num_scalar_prefetch=4, grid=(n_tiles // INVMAP_TILES,),
            in_specs=[pl.BlockSpec((n_tt, N_EXPERTS, tm), lambda j, *_: (0, 0, 0))],
            out_specs=[spec, spec]),
        out_shape=[jax.ShapeDtypeStruct((n_tiles * te, 1), jnp.int32)] * 2,
        compiler_params=_params("parallel"),
        name="invmap",
    )(tile_expert, rank0, lo, hi, code)
    return gid.reshape(n_tiles, 1, te), sid.reshape(n_tiles, 1, te)


def _moe_kernel(te_ref, nu_ref, gid0_ref, gid1_ref, gidn_ref, sidp_ref, sidc_ref, x1r_ref, wgu_ref, bgu_ref, wd_ref, bd_ref,
                y4_ref, xg, yb, gsem, ssem, *, te):
    j = pl.program_id(0)
    n_used = nu_ref[0]
    slot = j % 3
    ahead = (j + 2) % 3
    pending = (j + 1) % 3
    out_slot = j % 3
    prev_slot = (j + 2) % 3
    rows = te * ROW_SUBLANES
    n_rows = y4_ref.shape[0] - 2 * rows

    def gather_row(ids_ref, par, r):
        src = pl.multiple_of(ids_ref[0, 0, r] * ROW_SUBLANES, ROW_SUBLANES)
        return pltpu.make_async_copy(x1r_ref.at[pl.ds(src, ROW_SUBLANES), :],
                                     xg.at[par, pl.ds(r * ROW_SUBLANES, ROW_SUBLANES), :], gsem.at[par])

    def scatter_row(ids_ref, par, r):
        dst = pl.multiple_of(ids_ref[0, 0, r] * ROW_SUBLANES, ROW_SUBLANES)
        return pltpu.make_async_copy(yb.at[par, pl.ds(r * ROW_SUBLANES, ROW_SUBLANES), :],
                                     y4_ref.at[pl.ds(dst, ROW_SUBLANES), :], ssem.at[par])

    def wait_gather(par):
        pltpu.make_async_copy(x1r_ref.at[pl.ds(0, rows), :], xg.at[par], gsem.at[par]).wait()

    def wait_scatter(par):
        pltpu.make_async_copy(yb.at[par], y4_ref.at[pl.ds(0, rows), :], ssem.at[par]).wait()

    def step():
        @pl.when(j == 0)
        def _():
            yb[...] = jnp.zeros_like(yb)
            for region in range(2):
                spare = pltpu.make_async_copy(yb.at[region], y4_ref.at[pl.ds(n_rows + region * rows, rows), :],
                                              ssem.at[region])
                spare.start()
                spare.wait()

            def first(r, c):
                gather_row(gid0_ref, 0, r).start()
                gather_row(gid1_ref, 1, r).start()
                return c

            lax.fori_loop(0, te, first, 0)

        @pl.when(j >= 2)
        def _():
            wait_scatter(out_slot)

        wait_gather(slot)

        x = jnp.concatenate([xg[slot, pl.ds(s, te, stride=ROW_SUBLANES), :].astype(BF16)
                             for s in range(ROW_SUBLANES)], axis=1)
        for r in range(te):
            gather_row(gidn_ref, ahead, r).start(priority=r % DMA_PRIORITIES)
        for r in range(te):
            scatter_row(sidp_ref, prev_slot, r).start(priority=r % DMA_PRIORITIES)
        hg = _dot(x, wgu_ref[:, :D_FF]) + bgu_ref[:, :D_FF]
        runtime_zero = jnp.minimum(te_ref[j], 0)
        probe = pltpu.bitcast(xg[slot + runtime_zero, 0:ROW_SUBLANES, :], jnp.int32) & runtime_zero
        plus_zero = pltpu.bitcast(probe, F32)[0:1, :]
        bias_up = bgu_ref[:, D_FF:] + jnp.concatenate([plus_zero] * (D_FF // LANES), axis=1)
        hu = _dot(x, wgu_ref[:, D_FF:]) + bias_up
        gate = jnp.minimum(hg, SWIGLU_LIMIT)
        up = jnp.clip(hu, -SWIGLU_LIMIT, SWIGLU_LIMIT)
        act = (up + 1.0) * gate * jax.nn.sigmoid(SWIGLU_ALPHA * gate)
        y = _dot(act.astype(BF16), wd_ref[...]) + bd_ref[...]
        for s in range(ROW_SUBLANES):
            yb[out_slot, pl.ds(s, te, stride=ROW_SUBLANES), :] = y[:, s * LANES:(s + 1) * LANES]

        @pl.when(j == n_used - 1)
        def _():
            wait_scatter((j + 1) % 3)
            wait_scatter(prev_slot)
            wait_gather(pending)
            wait_gather(ahead)

            def last(r, c):
                scatter_row(sidc_ref, out_slot, r).start()
                return c

            lax.fori_loop(0, te, last, 0)
            wait_scatter(out_slot)

    pl.when(j < n_used)(step)


def _moe(tile_expert, n_used, gid, sid, x1r, w_gate_up, b_gate_up, w_down, b_down, n_tokens):
    n_tiles, _, te = gid.shape
    D = D_MODEL
    spare = (n_tokens * TOP_K + te + jnp.arange(te, dtype=jnp.int32)).reshape(1, 1, te)
    sid_prev = jnp.concatenate([spare, sid], axis=0)
    last = n_tiles - 1

    def ids(index):
        return pl.BlockSpec((1, 1, te), index, memory_space=pltpu.SMEM)

    return pl.pallas_call(
        functools.partial(_moe_kernel, te=te),
        grid_spec=pltpu.PrefetchScalarGridSpec(
            num_scalar_prefetch=2,
            grid=(n_tiles,),
            in_specs=[
                ids(lambda j, e, nu: (0, 0, 0)),
                ids(lambda j, e, nu: (1, 0, 0)),
                ids(lambda j, e, nu: (jnp.minimum(j + 2, last), 0, 0)),
                ids(lambda j, e, nu: (j, 0, 0)),
                ids(lambda j, e, nu: (j + 1, 0, 0)),
                pl.BlockSpec(memory_space=pl.ANY),
                pl.BlockSpec((None, D, 2 * D_FF), lambda j, e, nu: (e[j], 0, 0)),
                pl.BlockSpec((None, 1, 2 * D_FF), lambda j, e, nu: (e[j], 0, 0)),
                pl.BlockSpec((None, D_FF, D), lambda j, e, nu: (e[j], 0, 0)),
                pl.BlockSpec((None, 1, D), lambda j, e, nu: (e[j], 0, 0)),
            ],
            out_specs=pl.BlockSpec(memory_space=pl.ANY),
            scratch_shapes=[pltpu.VMEM((3, te * ROW_SUBLANES, LANES), F32),
                            pltpu.VMEM((3, te * ROW_SUBLANES, LANES), F32),
                            pltpu.SemaphoreType.DMA((3,)), pltpu.SemaphoreType.DMA((3,))],
        ),
        out_shape=jax.ShapeDtypeStruct(((n_tokens * TOP_K + 2 * te) * ROW_SUBLANES, LANES), F32),
        compiler_params=_params("arbitrary"),
        name="moe",
    )(tile_expert, n_used.reshape(1), gid, gid, gid, sid_prev, sid_prev, x1r,
      w_gate_up.astype(BF16), b_gate_up.reshape(N_EXPERTS, 1, 2 * D_FF).astype(F32),
      w_down.astype(BF16), b_down.reshape(N_EXPERTS, 1, D).astype(F32))


def _combine_kernel(x1_ref, p_ref, wt_ref, wpg_ref, wpp_ref, g_ref, b_ref, y0_ref, y1_ref, y2_ref, y3_ref,
                    x2_ref, *, tm):
    ple = (jax.nn.sigmoid(_dot(x1_ref[...].astype(BF16), wpg_ref[...]))
           * _dot(p_ref[...].astype(BF16), wpp_ref[...]))
    pieces = []
    for s in range(ROW_SUBLANES):
        acc = None
        for j, y_ref in enumerate((y0_ref, y1_ref, y2_ref, y3_ref)):
            term = wt_ref[:, j:j + 1] * y_ref[pl.ds(s, tm, stride=ROW_SUBLANES), :]
            acc = term if acc is None else acc + term
        pieces.append(acc)
    ffn = jnp.concatenate(pieces, axis=1)
    x2 = _layer_norm(DEEPNORM_ALPHA * x1_ref[...] + ffn + ple, g_ref[...], b_ref[...])
    x2_ref[...] = x2


def _combine(x1, p, wt, w_ple_gate, w_ple_proj, ln_g, ln_b, y4):
    T, D = x1.shape
    tm = min(WIDE_TILE, T)
    n_tt = T // tm
    weights = [w_ple_gate.astype(BF16), w_ple_proj.astype(BF16),
               ln_g.reshape(1, D).astype(F32), ln_b.reshape(1, D).astype(F32)]

    def choice(j):
        return pl.BlockSpec((tm * ROW_SUBLANES, LANES), lambda i: (j * n_tt + i, 0))

    return pl.pallas_call(
        functools.partial(_combine_kernel, tm=tm),
        grid=(n_tt,),
        in_specs=[_rows(tm, D), _rows(tm, PLE_DIM), _rows(tm, TOP_K)]
                 + [_resident(w.shape) for w in weights] + [choice(j) for j in range(TOP_K)],
        out_specs=_rows(tm, D),
        out_shape=jax.ShapeDtypeStruct((T, D), F32),
        compiler_params=_params("parallel"),
        name="combine",
    )(x1, p, wt, *weights, y4, y4, y4, y4)


def _layer(x, p, batch, seq, w):
    T, D = x.shape
    gu, vn, q, k, val, rs, la, sga, sgb = _inproj(x, w["w_in"], w["w_gate_a"], w["b_gate_a"],
                                                  w["sgu_ln_g"], w["sgu_ln_b"])
    o = _gla(q, k, val, la, rs, w["gla_norm_g"], batch, seq)
    x1, x1b, x1r = _merge(x, gu, vn, o, sga, sgb, w["sgu_w"], w["sgu_b"], w["w_branch_a"], w["w_branch_b"],
                          w["w_out"], w["ln1_g"], w["ln1_b"])
    wt, counts, prefix, code = _router(x1, x1b, w["w_router"], w["b_router"])
    te = EXPERT_TILE
    n_tiles = (T * TOP_K) // te + N_EXPERTS
    tile_expert, rank0, lo, hi, n_used = _tile_plan(counts[0, :N_EXPERTS], prefix[:, 0, :N_EXPERTS], n_tiles, te)
    gid, sid = _invmap(tile_expert, rank0, lo, hi, code, n_tiles, te, T)
    y4 = _moe(tile_expert, n_used, gid, sid, x1r, w["w_gate_up"], w["b_gate_up"], w["w_down"], w["b_down"], T)
    return _combine(x1, p, wt, w["w_ple_gate"], w["w_ple_proj"], w["ln2_g"], w["ln2_b"], y4)


_WEIGHT_NAMES = ("w_in", "w_gate_a", "b_gate_a", "sgu_ln_g", "sgu_ln_b", "sgu_w", "sgu_b", "gla_norm_g",
                 "w_branch_a", "w_branch_b", "w_out", "ln1_g", "ln1_b", "w_router", "b_router", "w_gate_up",
                 "b_gate_up", "w_down", "b_down", "w_ple_proj", "w_ple_gate", "ln2_g", "ln2_b")


def kernel(x, p, w_in, w_gate_a, b_gate_a, sgu_ln_g, sgu_ln_b, sgu_w, sgu_b, gla_norm_g, w_branch_a, w_branch_b,
           w_out, ln1_g, ln1_b, w_router, b_router, w_gate_up, b_gate_up, w_down, b_down, w_ple_proj, w_ple_gate,
           ln2_g, ln2_b):
    stacked = dict(zip(_WEIGHT_NAMES, (
        w_in, w_gate_a, b_gate_a, sgu_ln_g, sgu_ln_b, sgu_w, sgu_b, gla_norm_g, w_branch_a, w_branch_b, w_out,
        ln1_g, ln1_b, w_router, b_router, w_gate_up, b_gate_up, w_down, b_down, w_ple_proj, w_ple_gate,
        ln2_g, ln2_b)))
    batch, seq, d = x.shape
    depth = w_in.shape[0]
    xf = x.reshape(batch * seq, d)
    for i in range(depth):
        w = {name: arr[i] for name, arr in stacked.items()}
        xf = _layer(xf, p[i].reshape(batch * seq, PLE_DIM), batch, seq, w)
    return xf.reshape(batch, seq, d)
```
